```python
import jax, jax.numpy as jnp
from jax import lax
import numpy as np

D_MODEL = 1024
BATCH = 16
SEQ = 4096
DEPTH = 1

PLE_DIM = 256
MIX_WIDTH = D_MODEL
CONV_WIDTH = MIX_WIDTH // 2
GROUP_DIM = 64
N_CONV_GROUPS = CONV_WIDTH // GROUP_DIM
CONV_K = 3
ATTN_WIDTH = MIX_WIDTH - CONV_WIDTH
HEAD_DIM = GROUP_DIM
N_ATTN_HEADS = ATTN_WIDTH // HEAD_DIM
D_FF = -(-8 * D_MODEL // (3 * 256)) * 256
Q_BLOCK = 128
EPS = 1e-6
IN_COLS = 3 * CONV_WIDTH + 3 * ATTN_WIDTH + N_ATTN_HEADS

kernel_name = "hybrid_conv_forgetting_attn_ple_layer"


def rms_norm(x, g):
    xf = x.astype(jnp.float32)
    y = xf * lax.rsqrt(jnp.mean(xf * xf, axis=-1, keepdims=True) + EPS)
    return (y * g.astype(jnp.float32)).astype(x.dtype)


def group_rms_norm(y, g):
    B, S, W = y.shape
    yf = y.astype(jnp.float32).reshape(B, S, W // GROUP_DIM, GROUP_DIM)
    yf = yf * lax.rsqrt(jnp.mean(yf * yf, axis=-1, keepdims=True) + EPS)
    return (yf.reshape(B, S, W) * g.astype(jnp.float32)).astype(y.dtype)


def causal_depthwise_conv(u, w):
    S = u.shape[1]
    u_pad = jnp.pad(u, ((0, 0), (CONV_K - 1, 0), (0, 0)))
    return sum(w[j] * u_pad[:, j:j + S] for j in range(CONV_K))


def forgetting_attention(q, k, v, log_f):
    B, S, H, dh = q.shape
    nb = S // Q_BLOCK
    c = jnp.cumsum(log_f, axis=1)
    c_k = c.transpose(0, 2, 1)
    qf = q.astype(jnp.float32) * (dh ** -0.5)
    kf = k.astype(jnp.float32)
    vf = v.astype(jnp.float32)
    q_blocks = qf.reshape(B, nb, Q_BLOCK, H, dh).transpose(1, 0, 2, 3, 4)
    c_blocks = c.reshape(B, nb, Q_BLOCK, H).transpose(1, 0, 3, 2)
    k_pos = jnp.arange(S)

    def one_block(args):
        q_blk, c_q, blk = args
        s = jnp.einsum('bqhd,bkhd->bhqk', q_blk, kf)
        s = s + c_q[..., :, None] - c_k[:, :, None, :]
        q_pos = blk * Q_BLOCK + jnp.arange(Q_BLOCK)
        causal = k_pos[None, :] <= q_pos[:, None]
        s = jnp.where(causal, s, -jnp.inf)
        pr = jax.nn.softmax(s, axis=-1)
        return jnp.einsum('bhqk,bkhd->bqhd', pr, vf)

    out = lax.map(one_block, (q_blocks, c_blocks, jnp.arange(nb)))
    return out.transpose(1, 0, 2, 3, 4).reshape(B, S, H, dh).astype(v.dtype)


def setup_inputs(seed: int = 0) -> dict:
    key = jax.random.key(seed)
    ks = jax.random.split(key, 20)
    f32 = jnp.float32
    nrm = lambda k, shape, fan_in: jax.random.normal(k, shape, f32) * (fan_in ** -0.5)
    gain = lambda k, shape: 1.0 + 0.05 * jax.random.normal(k, shape, f32)
    x = jax.random.normal(ks[0], (BATCH, SEQ, D_MODEL), f32)
    p = jax.random.normal(ks[1], (DEPTH, BATCH, SEQ, PLE_DIM), f32)
    mix_norm = gain(ks[2], (DEPTH, D_MODEL))
    w_in = nrm(ks[3], (DEPTH, D_MODEL, IN_COLS), D_MODEL)
    b_f = (jnp.linspace(1.0, 6.0, N_ATTN_HEADS, dtype=f32)[None, :]
           + 0.01 * jax.random.normal(ks[4], (DEPTH, N_ATTN_HEADS), f32))
    conv_w = nrm(ks[5], (DEPTH, CONV_K, CONV_WIDTH), CONV_K)
    mix_out_norm = gain(ks[6], (DEPTH, MIX_WIDTH))
    w_o = nrm(ks[7], (DEPTH, MIX_WIDTH, D_MODEL), MIX_WIDTH)
    ffn_norm = gain(ks[8], (DEPTH, D_MODEL))
    w_gate_up = nrm(ks[9], (DEPTH, D_MODEL, 2 * D_FF), D_MODEL)
    w_down = nrm(ks[10], (DEPTH, D_FF, D_MODEL), D_FF)
    ple_norm = gain(ks[11], (DEPTH, D_MODEL))
    w_ple_gate = nrm(ks[12], (DEPTH, D_MODEL, D_MODEL), D_MODEL)
    b_ple_gate = 0.01 * jax.random.normal(ks[13], (DEPTH, D_MODEL), f32)
    w_ple_proj = nrm(ks[14], (DEPTH, PLE_DIM, D_MODEL), PLE_DIM)
    final_norm = gain(ks[15], (D_MODEL,))
    return {"x": x, "p": p, "mix_norm": mix_norm, "w_in": w_in, "b_f": b_f,
            "conv_w": conv_w, "mix_out_norm": mix_out_norm, "w_o": w_o,
            "ffn_norm": ffn_norm, "w_gate_up": w_gate_up, "w_down": w_down,
            "ple_norm": ple_norm, "w_ple_gate": w_ple_gate, "b_ple_gate": b_ple_gate,
            "w_ple_proj": w_ple_proj, "final_norm": final_norm}


def reference(x, p, mix_norm, w_in, b_f, conv_w, mix_out_norm, w_o, ffn_norm,
              w_gate_up, w_down, ple_norm, w_ple_gate, b_ple_gate, w_ple_proj,
              final_norm):
    B, S, _ = x.shape
    o_b = 0
    o_c = o_b + CONV_WIDTH
    o_u = o_c + CONV_WIDTH
    o_q = o_u + CONV_WIDTH
    o_k = o_q + ATTN_WIDTH
    o_v = o_k + ATTN_WIDTH
    o_f = o_v + ATTN_WIDTH
    h = x
    for i in range(DEPTH):
        xn = rms_norm(h, mix_norm[i])
        z = jnp.einsum('bsd,de->bse', xn, w_in[i])
        gate_b = z[..., o_b:o_c]
        gate_c = z[..., o_c:o_u]
        u = z[..., o_u:o_q]
        y_conv = gate_b * causal_depthwise_conv(gate_c * u, conv_w[i])
        q = z[..., o_q:o_k].reshape(B, S, N_ATTN_HEADS, HEAD_DIM)
        k = z[..., o_k:o_v].reshape(B, S, N_ATTN_HEADS, HEAD_DIM)
        v = z[..., o_v:o_f].reshape(B, S, N_ATTN_HEADS, HEAD_DIM)
        log_f = jax.nn.log_sigmoid(z[..., o_f:].astype(jnp.float32)
                                   + b_f[i].astype(jnp.float32))
        y_attn = forgetting_attention(q, k, v, log_f).reshape(B, S, ATTN_WIDTH)
        y = group_rms_norm(jnp.concatenate([y_conv, y_attn], axis=-1), mix_out_norm[i])
        h = h + jnp.einsum('bse,ed->bsd', y, w_o[i])
        gu = jnp.einsum('bsd,df->bsf', rms_norm(h, ffn_norm[i]), w_gate_up[i])
        g, up = gu[..., :D_FF], gu[..., D_FF:]
        h = h + jnp.einsum('bsf,fd->bsd', jax.nn.silu(g) * up, w_down[i])
        gate = jax.nn.sigmoid(jnp.einsum('bsd,de->bse', rms_norm(h, ple_norm[i]), w_ple_gate[i])
                              + b_ple_gate[i])
        h = h + gate * jnp.einsum('bsp,pd->bsd', p[i], w_ple_proj[i])
    return rms_norm(h, final_norm)
```

```python
import functools

import jax
import jax.numpy as jnp
from jax import lax
from jax.experimental import pallas as pl
from jax.experimental.pallas import tpu as pltpu

D_MODEL = 1024
PLE_DIM = 256
CONV_WIDTH = 512
ATTN_WIDTH = 512
GROUP_DIM = 64
N_HEADS = ATTN_WIDTH // GROUP_DIM
CONV_K = 3
D_FF = 2816
EPS = 1e-6

LANES = 128
SUBLANES = 8
HEADS_PER_BLOCK = LANES // GROUP_DIM
VMEM_LIMIT_BYTES = 56 * 1024 * 1024

TM_IN = 512
TQ = 256
TK = 256
TM_OUT = 512
FF_CHUNK = 256
MASK_VALUE = -1e30

F32 = jnp.float32
BF16 = jnp.bfloat16


def _rms_scale(x):
    return lax.rsqrt(jnp.mean(x * x, axis=-1, keepdims=True) + EPS)


def _dot(a, b):
    return jnp.dot(a, b, preferred_element_type=F32)


def _in_proj_kernel(x_ref, g_ref, w_ref, bf_ref, cw_ref, gmix_ref, gmat_ref,
                    yc_ref, q_ref, k_ref, v_ref, crow_ref,
                    conv_carry, c_carry):
    tm = x_ref.shape[0]

    @pl.when(pl.program_id(1) == 0)
    def _():
        conv_carry[...] = jnp.zeros_like(conv_carry)
        c_carry[...] = jnp.zeros_like(c_carry)

    x = x_ref[...]
    xn = (x * _rms_scale(x) * g_ref[...]).astype(BF16)

    cw = CONV_WIDTH
    gate_b = _dot(xn, w_ref[:, 0:cw])
    gate_c = _dot(xn, w_ref[:, cw:2 * cw])
    u = _dot(xn, w_ref[:, 2 * cw:3 * cw])
    gcu = gate_c * u
    carry = conv_carry[...]
    prev1 = pltpu.roll(gcu, 1, axis=0)
    prev2 = pltpu.roll(gcu, 2, axis=0)
    row8 = lax.broadcasted_iota(jnp.int32, (SUBLANES, cw), 0)
    head1 = jnp.where(row8 < 1, pltpu.roll(carry, 1, axis=0), prev1[0:SUBLANES])
    head2 = jnp.where(row8 < 2, pltpu.roll(carry, 2, axis=0), prev2[0:SUBLANES])
    prev1 = jnp.concatenate([head1, prev1[SUBLANES:]], axis=0)
    prev2 = jnp.concatenate([head2, prev2[SUBLANES:]], axis=0)
    conv_carry[...] = gcu[tm - SUBLANES:tm]
    conv = cw_ref[0:1, :] * prev2 + cw_ref[1:2, :] * prev1 + cw_ref[2:3, :] * gcu
    yc = gate_b * conv
    ms = _dot((yc * yc).astype(BF16), gmat_ref[...])
    yc_ref[...] = (yc * lax.rsqrt(ms + EPS) * gmix_ref[...]).astype(BF16)

    o_q = 3 * cw
    q_ref[...] = (_dot(xn, w_ref[:, o_q:o_q + ATTN_WIDTH]) * (GROUP_DIM ** -0.5)).astype(BF16)
    k_ref[...] = _dot(xn, w_ref[:, o_q + ATTN_WIDTH:o_q + 2 * ATTN_WIDTH]).astype(BF16)
    v_ref[...] = _dot(xn, w_ref[:, o_q + 2 * ATTN_WIDTH:o_q + 3 * ATTN_WIDTH]).astype(BF16)

    o_f = o_q + 3 * ATTN_WIDTH
    zf = _dot(xn, w_ref[:, o_f:o_f + LANES]) + bf_ref[...]
    lf = -(jnp.maximum(-zf, 0.0) + jnp.log1p(jnp.exp(-jnp.abs(zf))))
    lane = lax.broadcasted_iota(jnp.int32, (tm, LANES), 1)
    lf = jnp.where(lane < N_HEADS, lf, 0.0)
    r_i = lax.broadcasted_iota(jnp.int32, (tm, tm), 0)
    c_i = lax.broadcasted_iota(jnp.int32, (tm, tm), 1)
    tri = jnp.where(c_i <= r_i, 1.0, 0.0).astype(BF16)
    hi = lf.astype(BF16)
    r1 = lf - hi.astype(F32)
    mid = r1.astype(BF16)
    lo = (r1 - mid.astype(F32)).astype(BF16)
    c = c_carry[...] + ((_dot(tri, hi) + _dot(tri, mid)) + _dot(tri, lo))
    c_carry[...] = c[tm - 1:tm, :]
    crow_ref[...] = c.T[0:N_HEADS, :]


def _in_proj(x, g, w, bf, cw, gmix, gmat):
    B, S, _ = x.shape
    tm = TM_IN
    n_cols = w.shape[1]
    const = lambda b, i: (0, 0)
    single = dict(pipeline_mode=pl.Buffered(1))
    tok = lambda width: pl.BlockSpec((None, tm, width), lambda b, i: (b, i, 0))
    return pl.pallas_call(
        _in_proj_kernel,
        grid=(B, S // tm),
        in_specs=[
            tok(D_MODEL),
            pl.BlockSpec((1, D_MODEL), const, **single),
            pl.BlockSpec((D_MODEL, n_cols), const, **single),
            pl.BlockSpec((1, LANES), const, **single),
            pl.BlockSpec((CONV_K, CONV_WIDTH), const, **single),
            pl.BlockSpec((1, CONV_WIDTH), const, **single),
            pl.BlockSpec((CONV_WIDTH, CONV_WIDTH), const, **single),
        ],
        out_specs=[
            tok(CONV_WIDTH), tok(ATTN_WIDTH), tok(ATTN_WIDTH), tok(ATTN_WIDTH),
            pl.BlockSpec((None, N_HEADS, tm), lambda b, i: (b, 0, i)),
        ],
        out_shape=[
            jax.ShapeDtypeStruct((B, S, CONV_WIDTH), BF16),
            jax.ShapeDtypeStruct((B, S, ATTN_WIDTH), BF16),
            jax.ShapeDtypeStruct((B, S, ATTN_WIDTH), BF16),
            jax.ShapeDtypeStruct((B, S, ATTN_WIDTH), BF16),
            jax.ShapeDtypeStruct((B, N_HEADS, S), F32),
        ],
        scratch_shapes=[
            pltpu.VMEM((SUBLANES, CONV_WIDTH), F32),
            pltpu.VMEM((1, LANES), F32),
        ],
        compiler_params=pltpu.CompilerParams(
            dimension_semantics=("arbitrary", "arbitrary"),
            vmem_limit_bytes=VMEM_LIMIT_BYTES),
        name="in_proj",
    )(x, g, w, bf, cw, gmix, gmat)


def _attn_kernel(q_ref, k_ref, v_ref, crow_ref, gmix_ref, o_ref):
    S = q_ref.shape[0]
    nq = S // TQ
    lane = lax.broadcasted_iota(jnp.int32, (TQ, LANES), 1)
    r_i = lax.broadcasted_iota(jnp.int32, (TQ, TK), 0)
    c_i = lax.broadcasted_iota(jnp.int32, (TQ, TK), 1)
    causal = c_i <= r_i

    def q_block(qi, _):
        t0 = pl.multiple_of(qi * TQ, TQ)
        q = q_ref[pl.ds(t0, TQ), :]
        outs = []
        for hh in range(HEADS_PER_BLOCK):
            in_head = (lane >= hh * GROUP_DIM) & (lane < (hh + 1) * GROUP_DIM)
            qh = jnp.where(in_head, q, jnp.zeros_like(q))
            c_q = crow_ref[hh:hh + 1, pl.ds(t0, TQ)]
            cref = jnp.min(c_q, axis=-1, keepdims=True)

            def scores(s0):
                k = k_ref[pl.ds(s0, TK), :]
                s = lax.dot_general(qh, k, (((1,), (1,)), ((), ())),
                                    preferred_element_type=F32)
                return s + (cref - crow_ref[hh:hh + 1, pl.ds(s0, TK)])

            def update(s, s0, m, l, acc):
                m_new = jnp.maximum(m, jnp.max(s, axis=-1, keepdims=True))
                alpha = jnp.exp(m - m_new)
                p = jnp.exp(s - m_new)
                l = alpha * l + jnp.sum(p, axis=-1, keepdims=True)
                pv = _dot(p.astype(BF16), v_ref[pl.ds(s0, TK), :])
                return m_new, l, alpha * acc + pv

            def kv_block(kj, carry):
                s0 = pl.multiple_of(kj * TK, TK)
                return update(scores(s0), s0, *carry)

            init = (jnp.full((TQ, 1), MASK_VALUE, F32), jnp.zeros((TQ, 1), F32),
                    jnp.zeros((TQ, LANES), F32))
            m, l, acc = lax.fori_loop(0, qi, kv_block, init)
            s_diag = jnp.where(causal, scores(t0), MASK_VALUE)
            m, l, acc = update(s_diag, t0, m, l, acc)
            o = jnp.where(in_head, acc / l, 0.0)
            ms = jnp.sum(o * o, axis=-1, keepdims=True) * (1.0 / GROUP_DIM)
            outs.append(o * lax.rsqrt(ms + EPS))
        o_all = outs[0]
        for extra in outs[1:]:
            o_all = o_all + extra
        o_ref[pl.ds(t0, TQ), :] = (o_all * gmix_ref[...]).astype(BF16)
        return 0

    lax.fori_loop(0, nq, q_block, 0)


def _attention(q, k, v, crow, gmix_attn):
    B, S, _ = q.shape
    n_blk = N_HEADS // HEADS_PER_BLOCK
    seq = pl.BlockSpec((None, S, LANES), lambda b, j: (b, 0, j))
    return pl.pallas_call(
        _attn_kernel,
        grid=(B, n_blk),
        in_specs=[
            seq, seq, seq,
            pl.BlockSpec((None, None, HEADS_PER_BLOCK, S), lambda b, j: (b, j, 0, 0)),
            pl.BlockSpec((1, LANES), lambda b, j: (0, j)),
        ],
        out_specs=seq,
        out_shape=jax.ShapeDtypeStruct((B, S, ATTN_WIDTH), BF16),
        compiler_params=pltpu.CompilerParams(
            dimension_semantics=("arbitrary", "arbitrary"),
            vmem_limit_bytes=VMEM_LIMIT_BYTES),
        name="forgetting_attention",
    )(q, k, v, crow.reshape(B, n_blk, HEADS_PER_BLOCK, S), gmix_attn)


def _out_kernel(x_ref, yc_ref, ya_ref, p_ref, wo_ref, gffn_ref, wgu_ref, wd_ref,
                gple_ref, wpg_ref, bpg_ref, wpp_ref, gfin_ref, o_ref):
    y = jnp.concatenate([yc_ref[...], ya_ref[...]], axis=-1)
    h = x_ref[...] + _dot(y, wo_ref[...])
    hn = (h * _rms_scale(h) * gffn_ref[...]).astype(BF16)
    ff = None
    for c0 in range(0, D_FF, FF_CHUNK):
        g = _dot(hn, wgu_ref[:, c0:c0 + FF_CHUNK])
        up = _dot(hn, wgu_ref[:, D_FF + c0:D_FF + c0 + FF_CHUNK])
        a = (g * jax.nn.sigmoid(g) * up).astype(BF16)
        d = _dot(a, wd_ref[c0:c0 + FF_CHUNK, :])
        ff = d if ff is None else ff + d
    h = h + ff
    hn = (h * _rms_scale(h) * gple_ref[...]).astype(BF16)
    gate = jax.nn.sigmoid(_dot(hn, wpg_ref[...]) + bpg_ref[...])
    h = h + gate * _dot(p_ref[...].astype(BF16), wpp_ref[...])
    o_ref[...] = h * _rms_scale(h) * gfin_ref[...]


def _out_block(x, yc, ya, p, wo, gffn, wgu, wd, gple, wpg, bpg, wpp, gfin):
    n_tok = x.shape[0]
    tm = TM_OUT
    const = lambda i: (0, 0)
    single = dict(pipeline_mode=pl.Buffered(1))
    tok = lambda width: pl.BlockSpec((tm, width), lambda i: (i, 0))
    full = lambda a: pl.BlockSpec(a.shape, const, **single)
    return pl.pallas_call(
        _out_kernel,
        grid=(n_tok // tm,),
        in_specs=[tok(D_MODEL), tok(CONV_WIDTH), tok(ATTN_WIDTH), tok(PLE_DIM),
                  full(wo), full(gffn), full(wgu), full(wd), full(gple), full(wpg),
                  full(bpg), full(wpp), full(gfin)],
        out_specs=tok(D_MODEL),
        out_shape=jax.ShapeDtypeStruct((n_tok, D_MODEL), F32),
        compiler_params=pltpu.CompilerParams(
            dimension_semantics=("arbitrary",),
            vmem_limit_bytes=VMEM_LIMIT_BYTES),
        name="out_ffn_ple",
    )(x, yc, ya, p, wo, gffn, wgu, wd, gple, wpg, bpg, wpp, gfin)


def _layer(h, p_i, mix_norm, w_in, b_f, conv_w, mix_out_norm, w_o, ffn_norm, w_gate_up,
           w_down, ple_norm, w_ple_gate, b_ple_gate, w_ple_proj, out_norm):
    B, S, _ = h.shape
    row = lambda a: a.reshape(1, -1).astype(F32)
    n_main = 3 * CONV_WIDTH + 3 * ATTN_WIDTH
    w = jnp.pad(w_in, ((0, 0), (0, LANES - N_HEADS))).astype(BF16)
    bf = jnp.pad(row(b_f), ((0, 0), (0, LANES - N_HEADS)))
    assert w.shape[1] == n_main + LANES
    group = jnp.arange(CONV_WIDTH) // GROUP_DIM
    gmat = jnp.where(group[:, None] == group[None, :], 1.0 / GROUP_DIM, 0.0).astype(BF16)
    gmix = row(mix_out_norm)
    yc, q, k, v, crow = _in_proj(h, row(mix_norm), w, bf, conv_w.astype(F32),
                                 gmix[:, :CONV_WIDTH], gmat)
    ya = _attention(q, k, v, crow, gmix[:, CONV_WIDTH:])
    n_tok = B * S
    out = _out_block(
        h.reshape(n_tok, D_MODEL), yc.reshape(n_tok, CONV_WIDTH),
        ya.reshape(n_tok, ATTN_WIDTH), p_i.reshape(n_tok, PLE_DIM),
        w_o.astype(BF16), row(ffn_norm), w_gate_up.astype(BF16), w_down.astype(BF16),
        row(ple_norm), w_ple_gate.astype(BF16), row(b_ple_gate), w_ple_proj.astype(BF16),
        row(out_norm))
    return out.reshape(B, S, D_MODEL)


def kernel(x, p, mix_norm, w_in, b_f, conv_w, mix_out_norm, w_o, ffn_norm, w_gate_up,
           w_down, ple_norm, w_ple_gate, b_ple_gate, w_ple_proj, final_norm):
    depth = p.shape[0]
    assert depth == 1, "the final RMSNorm is fused into the single layer's last kernel"
    return _layer(x, p[0], mix_norm[0], w_in[0], b_f[0], conv_w[0], mix_out_norm[0],
                  w_o[0], ffn_norm[0], w_gate_up[0], w_down[0], ple_norm[0],
                  w_ple_gate[0], b_ple_gate[0], w_ple_proj[0], final_norm)
```

```python
import functools

import jax
import jax.numpy as jnp
from jax import lax
from jax.experimental import pallas as pl
from jax.experimental.pallas import tpu as pltpu

D_MODEL = 1024
PLE_DIM = 256
CONV_WIDTH = 512
ATTN_WIDTH = 512
GROUP_DIM = 64
N_HEADS = ATTN_WIDTH // GROUP_DIM
CONV_K = 3
D_FF = 2816
EPS = 1e-6

LANES = 128
SUBLANES = 8
HEADS_PER_BLOCK = LANES // GROUP_DIM
VMEM_LIMIT_BYTES = 56 * 1024 * 1024

TM_IN = 512
TQ = 512
TK = 256
TM_OUT = 512
FF_CHUNK = 256
MASK_VALUE = -1e30

F32 = jnp.float32
BF16 = jnp.bfloat16


def _rms_scale(x):
    return lax.rsqrt(jnp.mean(x * x, axis=-1, keepdims=True) + EPS)


def _dot(a, b):
    return jnp.dot(a, b, preferred_element_type=F32)


def _in_proj_kernel(x_ref, g_ref, w_ref, bf_ref, cw_ref, gmix_ref, gmat_ref,
                    yc_ref, q_ref, k_ref, v_ref, crow_ref,
                    conv_carry, c_carry):
    tm = x_ref.shape[0]

    @pl.when(pl.program_id(1) == 0)
    def _():
        conv_carry[...] = jnp.zeros_like(conv_carry)
        c_carry[...] = jnp.zeros_like(c_carry)

    x = x_ref[...]
    xn = (x * _rms_scale(x) * g_ref[...]).astype(BF16)

    cw = CONV_WIDTH
    gate_b = _dot(xn, w_ref[:, 0:cw])
    gate_c = _dot(xn, w_ref[:, cw:2 * cw])
    u = _dot(xn, w_ref[:, 2 * cw:3 * cw])
    gcu = gate_c * u
    carry = conv_carry[...]
    prev1 = pltpu.roll(gcu, 1, axis=0)
    prev2 = pltpu.roll(gcu, 2, axis=0)
    row8 = lax.broadcasted_iota(jnp.int32, (SUBLANES, cw), 0)
    head1 = jnp.where(row8 < 1, pltpu.roll(carry, 1, axis=0), prev1[0:SUBLANES])
    head2 = jnp.where(row8 < 2, pltpu.roll(carry, 2, axis=0), prev2[0:SUBLANES])
    prev1 = jnp.concatenate([head1, prev1[SUBLANES:]], axis=0)
    prev2 = jnp.concatenate([head2, prev2[SUBLANES:]], axis=0)
    conv_carry[...] = gcu[tm - SUBLANES:tm]
    conv = cw_ref[0:1, :] * prev2 + cw_ref[1:2, :] * prev1 + cw_ref[2:3, :] * gcu
    yc = gate_b * conv
    ms = _dot((yc * yc).astype(BF16), gmat_ref[...])
    yc_ref[...] = (yc * lax.rsqrt(ms + EPS) * gmix_ref[...]).astype(BF16)

    o_q = 3 * cw
    q_ref[...] = (_dot(xn, w_ref[:, o_q:o_q + ATTN_WIDTH]) * (GROUP_DIM ** -0.5)).astype(BF16)
    k_ref[...] = _dot(xn, w_ref[:, o_q + ATTN_WIDTH:o_q + 2 * ATTN_WIDTH]).astype(BF16)
    v_ref[...] = _dot(xn, w_ref[:, o_q + 2 * ATTN_WIDTH:o_q + 3 * ATTN_WIDTH]).astype(BF16)

    o_f = o_q + 3 * ATTN_WIDTH
    zf = _dot(xn, w_ref[:, o_f:o_f + LANES]) + bf_ref[...]
    lf = -(jnp.maximum(-zf, 0.0) + jnp.log1p(jnp.exp(-jnp.abs(zf))))
    lane = lax.broadcasted_iota(jnp.int32, (tm, LANES), 1)
    lf = jnp.where(lane < N_HEADS, lf, 0.0)
    r_i = lax.broadcasted_iota(jnp.int32, (tm, tm), 0)
    c_i = lax.broadcasted_iota(jnp.int32, (tm, tm), 1)
    tri = jnp.where(c_i <= r_i, 1.0, 0.0).astype(BF16)
    hi = lf.astype(BF16)
    r1 = lf - hi.astype(F32)
    mid = r1.astype(BF16)
    lo = (r1 - mid.astype(F32)).astype(BF16)
    c = c_carry[...] + ((_dot(tri, hi) + _dot(tri, mid)) + _dot(tri, lo))
    c_carry[...] = c[tm - 1:tm, :]
    crow_ref[...] = c.T[0:N_HEADS, :]


def _in_proj(x, g, w, bf, cw, gmix, gmat):
    B, S, _ = x.shape
    tm = TM_IN
    n_cols = w.shape[1]
    const = lambda b, i: (0, 0)
    single = dict(pipeline_mode=pl.Buffered(1))
    tok = lambda width: pl.BlockSpec((None, tm, width), lambda b, i: (b, i, 0))
    return pl.pallas_call(
        _in_proj_kernel,
        grid=(B, S // tm),
        in_specs=[
            tok(D_MODEL),
            pl.BlockSpec((1, D_MODEL), const, **single),
            pl.BlockSpec((D_MODEL, n_cols), const, **single),
            pl.BlockSpec((1, LANES), const, **single),
            pl.BlockSpec((CONV_K, CONV_WIDTH), const, **single),
            pl.BlockSpec((1, CONV_WIDTH), const, **single),
            pl.BlockSpec((CONV_WIDTH, CONV_WIDTH), const, **single),
        ],
        out_specs=[
            tok(CONV_WIDTH), tok(ATTN_WIDTH), tok(ATTN_WIDTH), tok(ATTN_WIDTH),
            pl.BlockSpec((None, N_HEADS, tm), lambda b, i: (b, 0, i)),
        ],
        out_shape=[
            jax.ShapeDtypeStruct((B, S, CONV_WIDTH), BF16),
            jax.ShapeDtypeStruct((B, S, ATTN_WIDTH), BF16),
            jax.ShapeDtypeStruct((B, S, ATTN_WIDTH), BF16),
            jax.ShapeDtypeStruct((B, S, ATTN_WIDTH), BF16),
            jax.ShapeDtypeStruct((B, N_HEADS, S), F32),
        ],
        scratch_shapes=[
            pltpu.VMEM((SUBLANES, CONV_WIDTH), F32),
            pltpu.VMEM((1, LANES), F32),
        ],
        compiler_params=pltpu.CompilerParams(
            dimension_semantics=("arbitrary", "arbitrary"),
            vmem_limit_bytes=VMEM_LIMIT_BYTES),
        name="in_proj",
    )(x, g, w, bf, cw, gmix, gmat)


def _attn_kernel(q_ref, k_ref, v_ref, crow_ref, gmix_ref, o_ref,
                 s_scr, acc_scr, mrun_scr, m_scr, q2_scr, vaug_scr):
    S = q_ref.shape[0]
    nq = S // TQ
    tiles_per_q = TQ // TK
    lane = lax.broadcasted_iota(jnp.int32, (TQ, LANES), 1)
    head_lanes = [(lane >= hh * GROUP_DIM) & (lane < (hh + 1) * GROUP_DIM)
                  for hh in range(HEADS_PER_BLOCK)]

    vaug_scr[:, 0:LANES] = v_ref[...]
    vaug_scr[:, LANES:] = jnp.ones((S, LANES), BF16)
    acc_scr[...] = jnp.zeros_like(acc_scr)
    mrun_scr[...] = jnp.full(mrun_scr.shape, MASK_VALUE, F32)

    def pass2_tile(s0):
        m = m_scr[...]
        s = s_scr[:, pl.ds(s0, TK)]
        p = jnp.concatenate(
            [jnp.exp(s[:, c0:c0 + LANES] - m) for c0 in range(0, TK, LANES)], axis=1)
        acc_scr[...] += _dot(p.astype(BF16), vaug_scr[pl.ds(s0, TK), :])

    def pass1_tile(s0, crefs, diag_offset=None):
        raw = lax.dot_general(q2_scr[...], k_ref[pl.ds(s0, TK), :],
                              (((1,), (1,)), ((), ())), preferred_element_type=F32)
        parts = []
        for hh in range(HEADS_PER_BLOCK):
            t = raw[hh * TQ:(hh + 1) * TQ] + (crefs[hh] - crow_ref[hh:hh + 1, pl.ds(s0, TK)])
            if diag_offset is not None:
                r_i = lax.broadcasted_iota(jnp.int32, (TQ, TK), 0)
                c_i = lax.broadcasted_iota(jnp.int32, (TQ, TK), 1)
                t = jnp.where(c_i + diag_offset <= r_i, t, MASK_VALUE)
            parts.append(t)
        t = jnp.concatenate(parts, axis=0)
        s_scr[:, pl.ds(s0, TK)] = t
        tmax = t[:, 0:LANES]
        for c0 in range(LANES, TK, LANES):
            tmax = jnp.maximum(tmax, t[:, c0:c0 + LANES])
        mrun_scr[...] = jnp.maximum(mrun_scr[...], tmax)

    def finalize(t0):
        out = None
        for hh in range(HEADS_PER_BLOCK):
            a = acc_scr[hh * TQ:(hh + 1) * TQ, :]
            o = jnp.where(head_lanes[hh], a[:, 0:LANES] / a[:, LANES:], 0.0)
            ms = jnp.sum(o * o, axis=-1, keepdims=True) * (1.0 / GROUP_DIM)
            o = o * lax.rsqrt(ms + EPS)
            out = o if out is None else out + o
        o_ref[pl.ds(t0, TQ), :] = (out * gmix_ref[...]).astype(BF16)
        acc_scr[...] = jnp.zeros_like(acc_scr)

    def q_block(qi, _):
        t0 = pl.multiple_of(qi * TQ, TQ)
        q = q_ref[pl.ds(t0, TQ), :]
        crefs = []
        for hh in range(HEADS_PER_BLOCK):
            q2_scr[hh * TQ:(hh + 1) * TQ, :] = jnp.where(head_lanes[hh], q, jnp.zeros_like(q))
            crefs.append(jnp.min(crow_ref[hh:hh + 1, pl.ds(t0, TQ)], axis=-1, keepdims=True))

        def both(j, _):
            s0 = pl.multiple_of(j * TK, TK)
            pass2_tile(s0)
            pass1_tile(s0, crefs)
            return 0

        lax.fori_loop(0, qi * tiles_per_q, both, 0)

        @pl.when(qi > 0)
        def _():
            finalize(pl.multiple_of(t0 - TQ, TQ))

        for d in range(tiles_per_q):
            pass1_tile(pl.multiple_of(t0 + d * TK, TK), crefs, diag_offset=d * TK)
        m = jnp.max(mrun_scr[...], axis=-1, keepdims=True)
        m_scr[...] = jnp.broadcast_to(m, m_scr.shape)
        mrun_scr[...] = jnp.full(mrun_scr.shape, MASK_VALUE, F32)
        return 0

    lax.fori_loop(0, nq, q_block, 0)

    def drain(j, _):
        pass2_tile(pl.multiple_of(j * TK, TK))
        return 0

    lax.fori_loop(0, nq * tiles_per_q, drain, 0)
    finalize(S - TQ)


def _attention(q, k, v, crow, gmix_attn):
    B, S, _ = q.shape
    n_blk = N_HEADS // HEADS_PER_BLOCK
    rows = HEADS_PER_BLOCK * TQ
    seq = pl.BlockSpec((None, S, LANES), lambda b, j: (b, 0, j))
    return pl.pallas_call(
        _attn_kernel,
        grid=(B, n_blk),
        in_specs=[
            seq, seq, seq,
            pl.BlockSpec((None, None, HEADS_PER_BLOCK, S), lambda b, j: (b, j, 0, 0)),
            pl.BlockSpec((1, LANES), lambda b, j: (0, j)),
        ],
        out_specs=seq,
        out_shape=jax.ShapeDtypeStruct((B, S, ATTN_WIDTH), BF16),
        scratch_shapes=[
            pltpu.VMEM((rows, S), F32),
            pltpu.VMEM((rows, 2 * LANES), F32),
            pltpu.VMEM((rows, LANES), F32),
            pltpu.VMEM((rows, LANES), F32),
            pltpu.VMEM((rows, LANES), BF16),
            pltpu.VMEM((S, 2 * LANES), BF16),
        ],
        compiler_params=pltpu.CompilerParams(
            dimension_semantics=("arbitrary", "arbitrary"),
            vmem_limit_bytes=VMEM_LIMIT_BYTES),
        name="forgetting_attention",
    )(q, k, v, crow.reshape(B, n_blk, HEADS_PER_BLOCK, S), gmix_attn)


def _out_kernel(x_ref, yc_ref, ya_ref, p_ref, wo_ref, gffn_ref, wgu_ref, wd_ref,
                gple_ref, wpg_ref, bpg_ref, wpp_ref, gfin_ref, o_ref):
    y = jnp.concatenate([yc_ref[...], ya_ref[...]], axis=-1)
    h = x_ref[...] + _dot(y, wo_ref[...])
    hn = (h * _rms_scale(h) * gffn_ref[...]).astype(BF16)
    ff = None
    for c0 in range(0, D_FF, FF_CHUNK):
        g = _dot(hn, wgu_ref[:, c0:c0 + FF_CHUNK])
        up = _dot(hn, wgu_ref[:, D_FF + c0:D_FF + c0 + FF_CHUNK])
        a = (g * jax.nn.sigmoid(g) * up).astype(BF16)
        d = _dot(a, wd_ref[c0:c0 + FF_CHUNK, :])
        ff = d if ff is None else ff + d
    h = h + ff
    hn = (h * _rms_scale(h) * gple_ref[...]).astype(BF16)
    gate = jax.nn.sigmoid(_dot(hn, wpg_ref[...]) + bpg_ref[...])
    h = h + gate * _dot(p_ref[...].astype(BF16), wpp_ref[...])
    o_ref[...] = h * _rms_scale(h) * gfin_ref[...]


def _out_block(x, yc, ya, p, wo, gffn, wgu, wd, gple, wpg, bpg, wpp, gfin):
    n_tok = x.shape[0]
    tm = TM_OUT
    const = lambda i: (0, 0)
    single = dict(pipeline_mode=pl.Buffered(1))
    tok = lambda width: pl.BlockSpec((tm, width), lambda i: (i, 0))
    full = lambda a: pl.BlockSpec(a.shape, const, **single)
    return pl.pallas_call(
        _out_kernel,
        grid=(n_tok // tm,),
        in_specs=[tok(D_MODEL), tok(CONV_WIDTH), tok(ATTN_WIDTH), tok(PLE_DIM),
                  full(wo), full(gffn), full(wgu), full(wd), full(gple), full(wpg),
                  full(bpg), full(wpp), full(gfin)],
        out_specs=tok(D_MODEL),
        out_shape=jax.ShapeDtypeStruct((n_tok, D_MODEL), F32),
        compiler_params=pltpu.CompilerParams(
            dimension_semantics=("arbitrary",),
            vmem_limit_bytes=VMEM_LIMIT_BYTES),
        name="out_ffn_ple",
    )(x, yc, ya, p, wo, gffn, wgu, wd, gple, wpg, bpg, wpp, gfin)


def _layer(h, p_i, mix_norm, w_in, b_f, conv_w, mix_out_norm, w_o, ffn_norm, w_gate_up,
           w_down, ple_norm, w_ple_gate, b_ple_gate, w_ple_proj, out_norm):
    B, S, _ = h.shape
    row = lambda a: a.reshape(1, -1).astype(F32)
    n_main = 3 * CONV_WIDTH + 3 * ATTN_WIDTH
    w = jnp.pad(w_in, ((0, 0), (0, LANES - N_HEADS))).astype(BF16)
    bf = jnp.pad(row(b_f), ((0, 0), (0, LANES - N_HEADS)))
    assert w.shape[1] == n_main + LANES
    group = jnp.arange(CONV_WIDTH) // GROUP_DIM
    gmat = jnp.where(group[:, None] == group[None, :], 1.0 / GROUP_DIM, 0.0).astype(BF16)
    gmix = row(mix_out_norm)
    yc, q, k, v, crow = _in_proj(h, row(mix_norm), w, bf, conv_w.astype(F32),
                                 gmix[:, :CONV_WIDTH], gmat)
    ya = _attention(q, k, v, crow, gmix[:, CONV_WIDTH:])
    n_tok = B * S
    out = _out_block(
        h.reshape(n_tok, D_MODEL), yc.reshape(n_tok, CONV_WIDTH),
        ya.reshape(n_tok, ATTN_WIDTH), p_i.reshape(n_tok, PLE_DIM),
        w_o.astype(BF16), row(ffn_norm), w_gate_up.astype(BF16), w_down.astype(BF16),
        row(ple_norm), w_ple_gate.astype(BF16), row(b_ple_gate), w_ple_proj.astype(BF16),
        row(out_norm))
    return out.reshape(B, S, D_MODEL)


def kernel(x, p, mix_norm, w_in, b_f, conv_w, mix_out_norm, w_o, ffn_norm, w_gate_up,
           w_down, ple_norm, w_ple_gate, b_ple_gate, w_ple_proj, final_norm):
    depth = p.shape[0]
    assert depth == 1, "the final RMSNorm is fused into the single layer's last kernel"
    return _layer(x, p[0], mix_norm[0], w_in[0], b_f[0], conv_w[0], mix_out_norm[0],
                  w_o[0], ffn_norm[0], w_gate_up[0], w_down[0], ple_norm[0],
                  w_ple_gate[0], b_ple_gate[0], w_ple_proj[0], final_norm)
```

```python
import functools

import jax
import jax.numpy as jnp
from jax import lax
from jax.experimental import pallas as pl
from jax.experimental.pallas import tpu as pltpu

D_MODEL = 1024
PLE_DIM = 256
CONV_WIDTH = 512
ATTN_WIDTH = 512
GROUP_DIM = 64
N_HEADS = ATTN_WIDTH // GROUP_DIM
CONV_K = 3
D_FF = 2816
EPS = 1e-6

LANES = 128
SUBLANES = 8
HEADS_PER_BLOCK = LANES // GROUP_DIM
VMEM_LIMIT_BYTES = 56 * 1024 * 1024

TM_IN = 512
TQ = 512
TK = 512
TM_OUT = 512
FF_CHUNK = 256
MASK_VALUE = -1e30

F32 = jnp.float32
BF16 = jnp.bfloat16


def _rms_scale(x):
    return lax.rsqrt(jnp.mean(x * x, axis=-1, keepdims=True) + EPS)


def _dot(a, b):
    return jnp.dot(a, b, preferred_element_type=F32)


def _in_proj_kernel(x_ref, g_ref, w_ref, bf_ref, cw_ref, gmix_ref, gmat_ref,
                    yc_ref, q_ref, k_ref, v_ref, crow_ref,
                    conv_carry, c_carry):
    tm = x_ref.shape[0]

    @pl.when(pl.program_id(1) == 0)
    def _():
        conv_carry[...] = jnp.zeros_like(conv_carry)
        c_carry[...] = jnp.zeros_like(c_carry)

    x = x_ref[...]
    xn = (x * _rms_scale(x) * g_ref[...]).astype(BF16)

    cw = CONV_WIDTH
    gate_b = _dot(xn, w_ref[:, 0:cw])
    gate_c = _dot(xn, w_ref[:, cw:2 * cw])
    u = _dot(xn, w_ref[:, 2 * cw:3 * cw])
    gcu = gate_c * u
    carry = conv_carry[...]
    prev1 = pltpu.roll(gcu, 1, axis=0)
    prev2 = pltpu.roll(gcu, 2, axis=0)
    row8 = lax.broadcasted_iota(jnp.int32, (SUBLANES, cw), 0)
    head1 = jnp.where(row8 < 1, pltpu.roll(carry, 1, axis=0), prev1[0:SUBLANES])
    head2 = jnp.where(row8 < 2, pltpu.roll(carry, 2, axis=0), prev2[0:SUBLANES])
    prev1 = jnp.concatenate([head1, prev1[SUBLANES:]], axis=0)
    prev2 = jnp.concatenate([head2, prev2[SUBLANES:]], axis=0)
    conv_carry[...] = gcu[tm - SUBLANES:tm]
    conv = cw_ref[0:1, :] * prev2 + cw_ref[1:2, :] * prev1 + cw_ref[2:3, :] * gcu
    yc = gate_b * conv
    ms = _dot((yc * yc).astype(BF16), gmat_ref[...])
    yc_ref[...] = (yc * lax.rsqrt(ms + EPS) * gmix_ref[...]).astype(BF16)

    o_q = 3 * cw
    q_ref[...] = (_dot(xn, w_ref[:, o_q:o_q + ATTN_WIDTH]) * (GROUP_DIM ** -0.5)).astype(BF16)
    k_ref[...] = _dot(xn, w_ref[:, o_q + ATTN_WIDTH:o_q + 2 * ATTN_WIDTH]).astype(BF16)
    o_v = o_q + 2 * ATTN_WIDTH
    zvf = _dot(xn, w_ref[:, o_v:o_v + ATTN_WIDTH + LANES])
    v_ref[...] = zvf[:, 0:ATTN_WIDTH].astype(BF16)

    zf = zvf[:, ATTN_WIDTH:] + bf_ref[...]
    lf = -(jnp.maximum(-zf, 0.0) + jnp.log1p(jnp.exp(-jnp.abs(zf))))
    lft = lf.T[0:N_HEADS, :]
    hi = lft.astype(BF16).astype(F32)
    r1 = lft - hi
    mid = r1.astype(BF16).astype(F32)
    lo = r1 - mid
    parts = jnp.concatenate([hi, mid, lo], axis=0).astype(BF16)
    r_i = lax.broadcasted_iota(jnp.int32, (tm, tm), 0)
    c_i = lax.broadcasted_iota(jnp.int32, (tm, tm), 1)
    triu = jnp.where(r_i <= c_i, 1.0, 0.0).astype(BF16)
    cs = _dot(parts, triu)
    c = c_carry[...] + ((cs[0:N_HEADS] + cs[N_HEADS:2 * N_HEADS]) + cs[2 * N_HEADS:])
    c_carry[...] = jnp.broadcast_to(c[:, tm - 1:tm], c_carry.shape)
    crow_ref[...] = c


def _in_proj(x, g, w, bf, cw, gmix, gmat):
    B, S, _ = x.shape
    tm = TM_IN
    n_cols = w.shape[1]
    const = lambda b, i: (0, 0)
    single = dict(pipeline_mode=pl.Buffered(1))
    tok = lambda width: pl.BlockSpec((None, tm, width), lambda b, i: (b, i, 0))
    return pl.pallas_call(
        _in_proj_kernel,
        grid=(B, S // tm),
        in_specs=[
            tok(D_MODEL),
            pl.BlockSpec((1, D_MODEL), const, **single),
            pl.BlockSpec((D_MODEL, n_cols), const, **single),
            pl.BlockSpec((1, LANES), const, **single),
            pl.BlockSpec((CONV_K, CONV_WIDTH), const, **single),
            pl.BlockSpec((1, CONV_WIDTH), const, **single),
            pl.BlockSpec((CONV_WIDTH, CONV_WIDTH), const, **single),
        ],
        out_specs=[
            tok(CONV_WIDTH), tok(ATTN_WIDTH), tok(ATTN_WIDTH), tok(ATTN_WIDTH),
            pl.BlockSpec((None, N_HEADS, tm), lambda b, i: (b, 0, i)),
        ],
        out_shape=[
            jax.ShapeDtypeStruct((B, S, CONV_WIDTH), BF16),
            jax.ShapeDtypeStruct((B, S, ATTN_WIDTH), BF16),
            jax.ShapeDtypeStruct((B, S, ATTN_WIDTH), BF16),
            jax.ShapeDtypeStruct((B, S, ATTN_WIDTH), BF16),
            jax.ShapeDtypeStruct((B, N_HEADS, S), F32),
        ],
        scratch_shapes=[
            pltpu.VMEM((SUBLANES, CONV_WIDTH), F32),
            pltpu.VMEM((N_HEADS, TM_IN), F32),
        ],
        compiler_params=pltpu.CompilerParams(
            dimension_semantics=("arbitrary", "arbitrary"),
            vmem_limit_bytes=VMEM_LIMIT_BYTES),
        name="in_proj",
    )(x, g, w, bf, cw, gmix, gmat)


def _attn_kernel(q_ref, k_ref, v_ref, crow_ref, gmix_ref, o_ref,
                 s_scr, acc_scr, mrun_scr, m_scr, q2_scr, vaug_scr):
    S = q_ref.shape[0]
    nq = S // TQ
    tiles_per_q = TQ // TK
    lane = lax.broadcasted_iota(jnp.int32, (TQ, LANES), 1)
    head_lanes = [(lane >= hh * GROUP_DIM) & (lane < (hh + 1) * GROUP_DIM)
                  for hh in range(HEADS_PER_BLOCK)]

    vaug_scr[:, 0:LANES] = v_ref[...]
    vaug_scr[:, LANES:] = jnp.ones((S, LANES), BF16)
    acc_scr[...] = jnp.zeros_like(acc_scr)
    mrun_scr[...] = jnp.full(mrun_scr.shape, MASK_VALUE, F32)

    def pass2_tile(s0):
        m = m_scr[...]
        s = s_scr[:, pl.ds(s0, TK)]
        p = jnp.concatenate(
            [jnp.exp(s[:, c0:c0 + LANES] - m) for c0 in range(0, TK, LANES)], axis=1)
        acc_scr[...] += _dot(p.astype(BF16), vaug_scr[pl.ds(s0, TK), :])

    def pass1_tile(s0, crefs, diag_offset=None):
        raw = lax.dot_general(q2_scr[...], k_ref[pl.ds(s0, TK), :],
                              (((1,), (1,)), ((), ())), preferred_element_type=F32)
        parts = []
        for hh in range(HEADS_PER_BLOCK):
            t = raw[hh * TQ:(hh + 1) * TQ] + (crefs[hh] - crow_ref[hh:hh + 1, pl.ds(s0, TK)])
            if diag_offset is not None:
                r_i = lax.broadcasted_iota(jnp.int32, (TQ, TK), 0)
                c_i = lax.broadcasted_iota(jnp.int32, (TQ, TK), 1)
                t = jnp.where(c_i + diag_offset <= r_i, t, MASK_VALUE)
            parts.append(t)
        t = jnp.concatenate(parts, axis=0)
        s_scr[:, pl.ds(s0, TK)] = t
        tmax = t[:, 0:LANES]
        for c0 in range(LANES, TK, LANES):
            tmax = jnp.maximum(tmax, t[:, c0:c0 + LANES])
        mrun_scr[...] = jnp.maximum(mrun_scr[...], tmax)

    def finalize(t0):
        out = None
        for hh in range(HEADS_PER_BLOCK):
            a = acc_scr[hh * TQ:(hh + 1) * TQ, :]
            o = jnp.where(head_lanes[hh], a[:, 0:LANES] / a[:, LANES:], 0.0)
            ms = jnp.sum(o * o, axis=-1, keepdims=True) * (1.0 / GROUP_DIM)
            o = o * lax.rsqrt(ms + EPS)
            out = o if out is None else out + o
        o_ref[pl.ds(t0, TQ), :] = (out * gmix_ref[...]).astype(BF16)
        acc_scr[...] = jnp.zeros_like(acc_scr)

    def q_block(qi, first):
        t0 = pl.multiple_of(qi * TQ, TQ)
        q = q_ref[pl.ds(t0, TQ), :]
        crefs = []
        for hh in range(HEADS_PER_BLOCK):
            q2_scr[hh * TQ:(hh + 1) * TQ, :] = jnp.where(head_lanes[hh], q, jnp.zeros_like(q))
            crefs.append(jnp.min(crow_ref[hh:hh + 1, pl.ds(t0, TQ)], axis=-1, keepdims=True))

        def both(j, _):
            s0 = pl.multiple_of(j * TK, TK)
            pass2_tile(s0)
            pass1_tile(s0, crefs)
            return 0

        if not first:
            lax.fori_loop(0, qi * tiles_per_q, both, 0)
            finalize(pl.multiple_of(t0 - TQ, TQ))

        for d in range(tiles_per_q):
            pass1_tile(pl.multiple_of(t0 + d * TK, TK), crefs, diag_offset=d * TK)
        m = jnp.max(mrun_scr[...], axis=-1, keepdims=True)
        m_scr[...] = jnp.broadcast_to(m, m_scr.shape)
        mrun_scr[...] = jnp.full(mrun_scr.shape, MASK_VALUE, F32)
        return 0

    q_block(0, True)
    lax.fori_loop(1, nq, lambda qi, _: q_block(qi, False), 0)

    def drain(j, _):
        pass2_tile(pl.multiple_of(j * TK, TK))
        return 0

    lax.fori_loop(0, nq * tiles_per_q, drain, 0)
    finalize(S - TQ)


def _attention(q, k, v, crow, gmix_attn):
    B, S, _ = q.shape
    n_blk = N_HEADS // HEADS_PER_BLOCK
    rows = HEADS_PER_BLOCK * TQ
    seq = pl.BlockSpec((None, S, LANES), lambda b, j: (b, 0, j))
    return pl.pallas_call(
        _attn_kernel,
        grid=(B, n_blk),
        in_specs=[
            seq, seq, seq,
            pl.BlockSpec((None, None, HEADS_PER_BLOCK, S), lambda b, j: (b, j, 0, 0)),
            pl.BlockSpec((1, LANES), lambda b, j: (0, j)),
        ],
        out_specs=seq,
        out_shape=jax.ShapeDtypeStruct((B, S, ATTN_WIDTH), BF16),
        scratch_shapes=[
            pltpu.VMEM((rows, S), F32),
            pltpu.VMEM((rows, 2 * LANES), F32),
            pltpu.VMEM((rows, LANES), F32),
            pltpu.VMEM((rows, LANES), F32),
            pltpu.VMEM((rows, LANES), BF16),
            pltpu.VMEM((S, 2 * LANES), BF16),
        ],
        compiler_params=pltpu.CompilerParams(
            dimension_semantics=("arbitrary", "arbitrary"),
            vmem_limit_bytes=VMEM_LIMIT_BYTES),
        name="forgetting_attention",
    )(q, k, v, crow.reshape(B, n_blk, HEADS_PER_BLOCK, S), gmix_attn)


def _out_kernel(x_ref, yc_ref, ya_ref, p_ref, wo_ref, gffn_ref, wgu_ref, wd_ref,
                gple_ref, wpg_ref, bpg_ref, wpp_ref, gfin_ref, o_ref):
    y = jnp.concatenate([yc_ref[...], ya_ref[...]], axis=-1)
    h = x_ref[...] + _dot(y, wo_ref[...])
    hn = (h * _rms_scale(h) * gffn_ref[...]).astype(BF16)
    ff = None
    for c0 in range(0, D_FF, FF_CHUNK):
        g = _dot(hn, wgu_ref[:, c0:c0 + FF_CHUNK])
        up = _dot(hn, wgu_ref[:, D_FF + c0:D_FF + c0 + FF_CHUNK])
        a = (g * jax.nn.sigmoid(g) * up).astype(BF16)
        d = _dot(a, wd_ref[c0:c0 + FF_CHUNK, :])
        ff = d if ff is None else ff + d
    h = h + ff
    hn = (h * _rms_scale(h) * gple_ref[...]).astype(BF16)
    gate = jax.nn.sigmoid(_dot(hn, wpg_ref[...]) + bpg_ref[...])
    h = h + gate * _dot(p_ref[...].astype(BF16), wpp_ref[...])
    o_ref[...] = h * _rms_scale(h) * gfin_ref[...]


def _out_block(x, yc, ya, p, wo, gffn, wgu, wd, gple, wpg, bpg, wpp, gfin):
    n_tok = x.shape[0]
    tm = TM_OUT
    const = lambda i: (0, 0)
    single = dict(pipeline_mode=pl.Buffered(1))
    tok = lambda width: pl.BlockSpec((tm, width), lambda i: (i, 0))
    full = lambda a: pl.BlockSpec(a.shape, const, **single)
    return pl.pallas_call(
        _out_kernel,
        grid=(n_tok // tm,),
        in_specs=[tok(D_MODEL), tok(CONV_WIDTH), tok(ATTN_WIDTH), tok(PLE_DIM),
                  full(wo), full(gffn), full(wgu), full(wd), full(gple), full(wpg),
                  full(bpg), full(wpp), full(gfin)],
        out_specs=tok(D_MODEL),
        out_shape=jax.ShapeDtypeStruct((n_tok, D_MODEL), F32),
        compiler_params=pltpu.CompilerParams(
            dimension_semantics=("arbitrary",),
            vmem_limit_bytes=VMEM_LIMIT_BYTES),
        name="out_ffn_ple",
    )(x, yc, ya, p, wo, gffn, wgu, wd, gple, wpg, bpg, wpp, gfin)


def _layer(h, p_i, mix_norm, w_in, b_f, conv_w, mix_out_norm, w_o, ffn_norm, w_gate_up,
           w_down, ple_norm, w_ple_gate, b_ple_gate, w_ple_proj, out_norm):
    B, S, _ = h.shape
    row = lambda a: a.reshape(1, -1).astype(F32)
    n_main = 3 * CONV_WIDTH + 3 * ATTN_WIDTH
    w = jnp.pad(w_in, ((0, 0), (0, LANES - N_HEADS))).astype(BF16)
    bf = jnp.pad(row(b_f), ((0, 0), (0, LANES - N_HEADS)))
    assert w.shape[1] == n_main + LANES
    group = jnp.arange(CONV_WIDTH) // GROUP_DIM
    gmat = jnp.where(group[:, None] == group[None, :], 1.0 / GROUP_DIM, 0.0).astype(BF16)
    gmix = row(mix_out_norm)
    yc, q, k, v, crow = _in_proj(h, row(mix_norm), w, bf, conv_w.astype(F32),
                                 gmix[:, :CONV_WIDTH], gmat)
    ya = _attention(q, k, v, crow, gmix[:, CONV_WIDTH:])
    n_tok = B * S
    out = _out_block(
        h.reshape(n_tok, D_MODEL), yc.reshape(n_tok, CONV_WIDTH),
        ya.reshape(n_tok, ATTN_WIDTH), p_i.reshape(n_tok, PLE_DIM),
        w_o.astype(BF16), row(ffn_norm), w_gate_up.astype(BF16), w_down.astype(BF16),
        row(ple_norm), w_ple_gate.astype(BF16), row(b_ple_gate), w_ple_proj.astype(BF16),
        row(out_norm))
    return out.reshape(B, S, D_MODEL)


def kernel(x, p, mix_norm, w_in, b_f, conv_w, mix_out_norm, w_o, ffn_norm, w_gate_up,
           w_down, ple_norm, w_ple_gate, b_ple_gate, w_ple_proj, final_norm):
    depth = p.shape[0]
    assert depth == 1, "the final RMSNorm is fused into the single layer's last kernel"
    return _layer(x, p[0], mix_norm[0], w_in[0], b_f[0], conv_w[0], mix_out_norm[0],
                  w_o[0], ffn_norm[0], w_gate_up[0], w_down[0], ple_norm[0],
                  w_ple_gate[0], b_ple_gate[0], w_ple_proj[0], final_norm)
```

```python
import jax
import jax.numpy as jnp
from jax import lax
from jax.experimental import pallas as pl
from jax.experimental.pallas import tpu as pltpu

D_MODEL = 1024
PLE_DIM = 256
CONV_WIDTH = 512
ATTN_WIDTH = 512
GROUP_DIM = 64
N_HEADS = ATTN_WIDTH // GROUP_DIM
CONV_K = 3
D_FF = 2816
EPS = 1e-6

LANES = 128
SUBLANES = 8
HEADS_PER_BLOCK = LANES // GROUP_DIM
VMEM_LIMIT_BYTES = 56 * 1024 * 1024

TM_IN = 512
TQ = 512
TK = 512
PAIRS_PER_STEP = 2
TM_OUT = 512
FF_CHUNK = 256
MASK_VALUE = -1e30

F32 = jnp.float32
BF16 = jnp.bfloat16


def _rms_scale(x):
    return lax.rsqrt(jnp.mean(x * x, axis=-1, keepdims=True) + EPS)


def _dot(a, b):
    return jnp.dot(a, b, preferred_element_type=F32)


def _in_proj_kernel(x_ref, g_ref, w_ref, bf_ref, cw_ref, gmix_ref, gmat_ref,
                    yc_ref, q_ref, k_ref, v_ref, crow_ref,
                    conv_carry, c_carry):
    tm = x_ref.shape[0]

    @pl.when(pl.program_id(1) == 0)
    def _():
        conv_carry[...] = jnp.zeros_like(conv_carry)
        c_carry[...] = jnp.zeros_like(c_carry)

    x = x_ref[...]
    xn = (x * _rms_scale(x) * g_ref[...]).astype(BF16)

    cw = CONV_WIDTH
    gate_b = _dot(xn, w_ref[:, 0:cw])
    gate_c = _dot(xn, w_ref[:, cw:2 * cw])
    u = _dot(xn, w_ref[:, 2 * cw:3 * cw])
    gcu = gate_c * u
    carry = conv_carry[...]
    prev1 = pltpu.roll(gcu, 1, axis=0)
    prev2 = pltpu.roll(gcu, 2, axis=0)
    row8 = lax.broadcasted_iota(jnp.int32, (SUBLANES, cw), 0)
    head1 = jnp.where(row8 < 1, pltpu.roll(carry, 1, axis=0), prev1[0:SUBLANES])
    head2 = jnp.where(row8 < 2, pltpu.roll(carry, 2, axis=0), prev2[0:SUBLANES])
    prev1 = jnp.concatenate([head1, prev1[SUBLANES:]], axis=0)
    prev2 = jnp.concatenate([head2, prev2[SUBLANES:]], axis=0)
    conv_carry[...] = gcu[tm - SUBLANES:tm]
    conv = cw_ref[0:1, :] * prev2 + cw_ref[1:2, :] * prev1 + cw_ref[2:3, :] * gcu
    yc = gate_b * conv
    ms = _dot((yc * yc).astype(BF16), gmat_ref[...])
    yc_ref[...] = (yc * lax.rsqrt(ms + EPS) * gmix_ref[...]).astype(BF16)

    o_q = 3 * cw
    q_ref[...] = (_dot(xn, w_ref[:, o_q:o_q + ATTN_WIDTH]) * (GROUP_DIM ** -0.5)).astype(BF16)
    k_ref[...] = _dot(xn, w_ref[:, o_q + ATTN_WIDTH:o_q + 2 * ATTN_WIDTH]).astype(BF16)
    o_v = o_q + 2 * ATTN_WIDTH
    zvf = _dot(xn, w_ref[:, o_v:o_v + ATTN_WIDTH + LANES])
    v_ref[...] = zvf[:, 0:ATTN_WIDTH].astype(BF16)

    zf = zvf[:, ATTN_WIDTH:] + bf_ref[...]
    lf = -(jnp.maximum(-zf, 0.0) + jnp.log1p(jnp.exp(-jnp.abs(zf))))
    lft = lf.T[0:N_HEADS, :]
    hi = lft.astype(BF16).astype(F32)
    r1 = lft - hi
    mid = r1.astype(BF16).astype(F32)
    lo = r1 - mid
    parts = jnp.concatenate([hi, mid, lo], axis=0).astype(BF16)
    r_i = lax.broadcasted_iota(jnp.int32, (tm, tm), 0)
    c_i = lax.broadcasted_iota(jnp.int32, (tm, tm), 1)
    triu = jnp.where(r_i <= c_i, 1.0, 0.0).astype(BF16)
    cs = _dot(parts, triu)
    c = c_carry[...] + ((cs[0:N_HEADS] + cs[N_HEADS:2 * N_HEADS]) + cs[2 * N_HEADS:])
    c_carry[...] = jnp.broadcast_to(c[:, tm - 1:tm], c_carry.shape)
    crow_ref[...] = c


def _in_proj(x, g, w, bf, cw, gmix, gmat):
    B, S, _ = x.shape
    tm = TM_IN
    n_cols = w.shape[1]
    const = lambda b, i: (0, 0)
    single = dict(pipeline_mode=pl.Buffered(1))
    tok = lambda width: pl.BlockSpec((None, tm, width), lambda b, i: (b, i, 0))
    return pl.pallas_call(
        _in_proj_kernel,
        grid=(B, S // tm),
        in_specs=[
            tok(D_MODEL),
            pl.BlockSpec((1, D_MODEL), const, **single),
            pl.BlockSpec((D_MODEL, n_cols), const, **single),
            pl.BlockSpec((1, LANES), const, **single),
            pl.BlockSpec((CONV_K, CONV_WIDTH), const, **single),
            pl.BlockSpec((1, CONV_WIDTH), const, **single),
            pl.BlockSpec((CONV_WIDTH, CONV_WIDTH), const, **single),
        ],
        out_specs=[
            tok(CONV_WIDTH), tok(ATTN_WIDTH), tok(ATTN_WIDTH), tok(ATTN_WIDTH),
            pl.BlockSpec((None, N_HEADS, tm), lambda b, i: (b, 0, i)),
        ],
        out_shape=[
            jax.ShapeDtypeStruct((B, S, CONV_WIDTH), BF16),
            jax.ShapeDtypeStruct((B, S, ATTN_WIDTH), BF16),
            jax.ShapeDtypeStruct((B, S, ATTN_WIDTH), BF16),
            jax.ShapeDtypeStruct((B, S, ATTN_WIDTH), BF16),
            jax.ShapeDtypeStruct((B, N_HEADS, S), F32),
        ],
        scratch_shapes=[
            pltpu.VMEM((SUBLANES, CONV_WIDTH), F32),
            pltpu.VMEM((N_HEADS, TM_IN), F32),
        ],
        compiler_params=pltpu.CompilerParams(
            dimension_semantics=("arbitrary", "arbitrary"),
            vmem_limit_bytes=VMEM_LIMIT_BYTES),
        name="in_proj",
    )(x, g, w, bf, cw, gmix, gmat)


def _attn_kernel(q_ref, k_ref, v_ref, crow_ref, gmix_ref, o_ref,
                 s_scr, acc_scr, mrun_scr, m_scr, q2_scr):
    S = q_ref.shape[0]
    nq = S // TQ
    tiles_per_q = TQ // TK
    pairs = range(PAIRS_PER_STEP)
    lane = lax.broadcasted_iota(jnp.int32, (TQ, LANES), 1)
    head_lanes = [(lane >= hh * GROUP_DIM) & (lane < (hh + 1) * GROUP_DIM)
                  for hh in range(HEADS_PER_BLOCK)]
    ones = jnp.ones((TK, LANES), BF16)

    acc_scr[...] = jnp.zeros_like(acc_scr)
    mrun_scr[...] = jnp.full(mrun_scr.shape, MASK_VALUE, F32)

    def pass2_tile(pp, s0):
        m = m_scr[pp]
        s = s_scr[pp, :, pl.ds(s0, TK)]
        p = jnp.concatenate(
            [jnp.exp(s[:, c0:c0 + LANES] - m) for c0 in range(0, TK, LANES)], axis=1)
        v_aug = jnp.concatenate([v_ref[pl.ds(s0, TK), pp * LANES:(pp + 1) * LANES], ones],
                                axis=1)
        acc_scr[pp] += _dot(p.astype(BF16), v_aug)

    def pass1_tile(pp, s0, crefs, diag_offset=None):
        raw = lax.dot_general(q2_scr[pp], k_ref[pl.ds(s0, TK), pp * LANES:(pp + 1) * LANES],
                              (((1,), (1,)), ((), ())), preferred_element_type=F32)
        parts = []
        for hh in range(HEADS_PER_BLOCK):
            t = raw[hh * TQ:(hh + 1) * TQ] + (crefs[pp][hh]
                                             - crow_ref[pp, hh:hh + 1, pl.ds(s0, TK)])
            if diag_offset is not None:
                r_i = lax.broadcasted_iota(jnp.int32, (TQ, TK), 0)
                c_i = lax.broadcasted_iota(jnp.int32, (TQ, TK), 1)
                t = jnp.where(c_i + diag_offset <= r_i, t, MASK_VALUE)
            parts.append(t)
        t = jnp.concatenate(parts, axis=0)
        s_scr[pp, :, pl.ds(s0, TK)] = t
        tmax = t[:, 0:LANES]
        for c0 in range(LANES, TK, LANES):
            tmax = jnp.maximum(tmax, t[:, c0:c0 + LANES])
        mrun_scr[pp] = jnp.maximum(mrun_scr[pp], tmax)

    def finalize(t0):
        for pp in pairs:
            out = None
            for hh in range(HEADS_PER_BLOCK):
                a = acc_scr[pp, hh * TQ:(hh + 1) * TQ, :]
                o = jnp.where(head_lanes[hh], a[:, 0:LANES] / a[:, LANES:], 0.0)
                ms = jnp.sum(o * o, axis=-1, keepdims=True) * (1.0 / GROUP_DIM)
                o = o * lax.rsqrt(ms + EPS)
                out = o if out is None else out + o
            o_ref[pl.ds(t0, TQ), pp * LANES:(pp + 1) * LANES] = (
                out * gmix_ref[:, pp * LANES:(pp + 1) * LANES]).astype(BF16)
        acc_scr[...] = jnp.zeros_like(acc_scr)

    def q_block(qi, first):
        t0 = pl.multiple_of(qi * TQ, TQ)
        crefs = []
        for pp in pairs:
            q = q_ref[pl.ds(t0, TQ), pp * LANES:(pp + 1) * LANES]
            crefs.append([])
            for hh in range(HEADS_PER_BLOCK):
                q2_scr[pp, hh * TQ:(hh + 1) * TQ, :] = jnp.where(head_lanes[hh], q,
                                                                 jnp.zeros_like(q))
                crefs[pp].append(jnp.min(crow_ref[pp, hh:hh + 1, pl.ds(t0, TQ)],
                                         axis=-1, keepdims=True))

        def both(j, _):
            s0 = pl.multiple_of(j * TK, TK)
            for pp in pairs:
                pass2_tile(pp, s0)
            for pp in pairs:
                pass1_tile(pp, s0, crefs)
            return 0

        if not first:
            lax.fori_loop(0, qi * tiles_per_q, both, 0)
            finalize(pl.multiple_of(t0 - TQ, TQ))

        for d in range(tiles_per_q):
            for pp in pairs:
                pass1_tile(pp, pl.multiple_of(t0 + d * TK, TK), crefs, diag_offset=d * TK)
        for pp in pairs:
            m = jnp.max(mrun_scr[pp], axis=-1, keepdims=True)
            m_scr[pp] = jnp.broadcast_to(m, m_scr.shape[1:])
        mrun_scr[...] = jnp.full(mrun_scr.shape, MASK_VALUE, F32)
        return 0

    q_block(0, True)
    lax.fori_loop(1, nq, lambda qi, _: q_block(qi, False), 0)

    def drain(j, _):
        for pp in pairs:
            pass2_tile(pp, pl.multiple_of(j * TK, TK))
        return 0

    lax.fori_loop(0, nq * tiles_per_q, drain, 0)
    finalize(S - TQ)


def _attention(q, k, v, crow, gmix_attn):
    B, S, _ = q.shape
    n_pairs = N_HEADS // HEADS_PER_BLOCK
    rows = HEADS_PER_BLOCK * TQ
    width = PAIRS_PER_STEP * LANES
    seq = pl.BlockSpec((None, S, width), lambda b, j: (b, 0, j))
    return pl.pallas_call(
        _attn_kernel,
        grid=(B, n_pairs // PAIRS_PER_STEP),
        in_specs=[
            seq, seq, seq,
            pl.BlockSpec((None, PAIRS_PER_STEP, HEADS_PER_BLOCK, S), lambda b, j: (b, j, 0, 0)),
            pl.BlockSpec((1, width), lambda b, j: (0, j)),
        ],
        out_specs=seq,
        out_shape=jax.ShapeDtypeStruct((B, S, ATTN_WIDTH), BF16),
        scratch_shapes=[
            pltpu.VMEM((PAIRS_PER_STEP, rows, S), F32),
            pltpu.VMEM((PAIRS_PER_STEP, rows, 2 * LANES), F32),
            pltpu.VMEM((PAIRS_PER_STEP, rows, LANES), F32),
            pltpu.VMEM((PAIRS_PER_STEP, rows, LANES), F32),
            pltpu.VMEM((PAIRS_PER_STEP, rows, LANES), BF16),
        ],
        compiler_params=pltpu.CompilerParams(
            dimension_semantics=("arbitrary", "arbitrary"),
            vmem_limit_bytes=VMEM_LIMIT_BYTES),
        name="forgetting_attention",
    )(q, k, v, crow.reshape(B, n_pairs, HEADS_PER_BLOCK, S), gmix_attn)


def _out_kernel(x_ref, yc_ref, ya_ref, p_ref, wo_ref, gffn_ref, wgu_ref, wd_ref,
                gple_ref, wpg_ref, bpg_ref, wpp_ref, gfin_ref, o_ref):
    y = jnp.concatenate([yc_ref[...], ya_ref[...]], axis=-1)
    h = x_ref[...] + _dot(y, wo_ref[...])
    hn = (h * _rms_scale(h) * gffn_ref[...]).astype(BF16)
    ff = None
    for c0 in range(0, D_FF, FF_CHUNK):
        g = _dot(hn, wgu_ref[:, c0:c0 + FF_CHUNK])
        up = _dot(hn, wgu_ref[:, D_FF + c0:D_FF + c0 + FF_CHUNK])
        a = (g * jax.nn.sigmoid(g) * up).astype(BF16)
        d = _dot(a, wd_ref[c0:c0 + FF_CHUNK, :])
        ff = d if ff is None else ff + d
    h = h + ff
    hn = (h * _rms_scale(h) * gple_ref[...]).astype(BF16)
    gate = jax.nn.sigmoid(_dot(hn, wpg_ref[...]) + bpg_ref[...])
    h = h + gate * _dot(p_ref[...].astype(BF16), wpp_ref[...])
    o_ref[...] = h * _rms_scale(h) * gfin_ref[...]


def _out_block(x, yc, ya, p, wo, gffn, wgu, wd, gple, wpg, bpg, wpp, gfin):
    n_tok = x.shape[0]
    tm = TM_OUT
    const = lambda i: (0, 0)
    single = dict(pipeline_mode=pl.Buffered(1))
    tok = lambda width: pl.BlockSpec((tm, width), lambda i: (i, 0))
    full = lambda a: pl.BlockSpec(a.shape, const, **single)
    return pl.pallas_call(
        _out_kernel,
        grid=(n_tok // tm,),
        in_specs=[tok(D_MODEL), tok(CONV_WIDTH), tok(ATTN_WIDTH), tok(PLE_DIM),
                  full(wo), full(gffn), full(wgu), full(wd), full(gple), full(wpg),
                  full(bpg), full(wpp), full(gfin)],
        out_specs=tok(D_MODEL),
        out_shape=jax.ShapeDtypeStruct((n_tok, D_MODEL), F32),
        compiler_params=pltpu.CompilerParams(
            dimension_semantics=("arbitrary",),
            vmem_limit_bytes=VMEM_LIMIT_BYTES),
        name="out_ffn_ple",
    )(x, yc, ya, p, wo, gffn, wgu, wd, gple, wpg, bpg, wpp, gfin)


def _layer(h, p_i, mix_norm, w_in, b_f, conv_w, mix_out_norm, w_o, ffn_norm, w_gate_up,
           w_down, ple_norm, w_ple_gate, b_ple_gate, w_ple_proj, out_norm):
    B, S, _ = h.shape
    row = lambda a: a.reshape(1, -1).astype(F32)
    n_main = 3 * CONV_WIDTH + 3 * ATTN_WIDTH
    w = jnp.pad(w_in, ((0, 0), (0, LANES - N_HEADS))).astype(BF16)
    bf = jnp.pad(row(b_f), ((0, 0), (0, LANES - N_HEADS)))
    assert w.shape[1] == n_main + LANES
    group = jnp.arange(CONV_WIDTH) // GROUP_DIM
    gmat = jnp.where(group[:, None] == group[None, :], 1.0 / GROUP_DIM, 0.0).astype(BF16)
    gmix = row(mix_out_norm)
    yc, q, k, v, crow = _in_proj(h, row(mix_norm), w, bf, conv_w.astype(F32),
                                 gmix[:, :CONV_WIDTH], gmat)
    ya = _attention(q, k, v, crow, gmix[:, CONV_WIDTH:])
    n_tok = B * S
    out = _out_block(
        h.reshape(n_tok, D_MODEL), yc.reshape(n_tok, CONV_WIDTH),
        ya.reshape(n_tok, ATTN_WIDTH), p_i.reshape(n_tok, PLE_DIM),
        w_o.astype(BF16), row(ffn_norm), w_gate_up.astype(BF16), w_down.astype(BF16),
        row(ple_norm), w_ple_gate.astype(BF16), row(b_ple_gate), w_ple_proj.astype(BF16),
        row(out_norm))
    return out.reshape(B, S, D_MODEL)


def kernel(x, p, mix_norm, w_in, b_f, conv_w, mix_out_norm, w_o, ffn_norm, w_gate_up,
           w_down, ple_norm, w_ple_gate, b_ple_gate, w_ple_proj, final_norm):
    depth = p.shape[0]
    assert depth == 1, "the final RMSNorm is fused into the single layer's last kernel"
    return _layer(x, p[0], mix_norm[0], w_in[0], b_f[0], conv_w[0], mix_out_norm[0],
                  w_o[0], ffn_norm[0], w_gate_up[0], w_down[0], ple_norm[0],
                  w_ple_gate[0], b_ple_gate[0], w_ple_proj[0], final_norm)
```

```python
import jax
import jax.numpy as jnp
from jax import lax
from jax.experimental import pallas as pl
from jax.experimental.pallas import tpu as pltpu

D_MODEL = 1024
PLE_DIM = 256
CONV_WIDTH = 512
ATTN_WIDTH = 512
GROUP_DIM = 64
N_HEADS = ATTN_WIDTH // GROUP_DIM
CONV_K = 3
D_FF = 2816
EPS = 1e-6

LANES = 128
SUBLANES = 8
HEADS_PER_BLOCK = LANES // GROUP_DIM
VMEM_LIMIT_BYTES = 56 * 1024 * 1024

TM_IN = 512
TQ = 512
TK = 512
PAIRS_PER_STEP = 2
DRAIN_UNROLL = 2
TM_OUT = 512
FF_CHUNK = 256
MASK_VALUE = -1e30

F32 = jnp.float32
BF16 = jnp.bfloat16


def _rms_scale(x):
    return lax.rsqrt(jnp.mean(x * x, axis=-1, keepdims=True) + EPS)


def _dot(a, b):
    return jnp.dot(a, b, preferred_element_type=F32)


def _in_proj_kernel(x_ref, g_ref, w_ref, bf_ref, cw_ref, gmix_ref, gmat_ref,
                    yc_ref, q_ref, k_ref, v_ref, crow_ref,
                    conv_carry, c_carry):
    tm = x_ref.shape[0]

    @pl.when(pl.program_id(1) == 0)
    def _():
        conv_carry[...] = jnp.zeros_like(conv_carry)
        c_carry[...] = jnp.zeros_like(c_carry)

    x = x_ref[...]
    xn = (x * _rms_scale(x) * g_ref[...]).astype(BF16)

    cw = CONV_WIDTH
    o_q = 3 * cw
    o_v = o_q + 2 * ATTN_WIDTH
    gate_c = _dot(xn, w_ref[:, cw:2 * cw])
    u = _dot(xn, w_ref[:, 2 * cw:3 * cw])
    zvf = _dot(xn, w_ref[:, o_v:o_v + ATTN_WIDTH + LANES])
    gate_b = _dot(xn, w_ref[:, 0:cw])
    q_ref[...] = (_dot(xn, w_ref[:, o_q:o_q + ATTN_WIDTH]) * (GROUP_DIM ** -0.5)).astype(BF16)
    v_ref[...] = zvf[:, 0:ATTN_WIDTH].astype(BF16)

    gcu = gate_c * u
    carry = conv_carry[...]
    prev1 = pltpu.roll(gcu, 1, axis=0)
    prev2 = pltpu.roll(gcu, 2, axis=0)
    row8 = lax.broadcasted_iota(jnp.int32, (SUBLANES, cw), 0)
    head1 = jnp.where(row8 < 1, pltpu.roll(carry, 1, axis=0), prev1[0:SUBLANES])
    head2 = jnp.where(row8 < 2, pltpu.roll(carry, 2, axis=0), prev2[0:SUBLANES])
    prev1 = jnp.concatenate([head1, prev1[SUBLANES:]], axis=0)
    prev2 = jnp.concatenate([head2, prev2[SUBLANES:]], axis=0)
    conv_carry[...] = gcu[tm - SUBLANES:tm]
    conv = cw_ref[0:1, :] * prev2 + cw_ref[1:2, :] * prev1 + cw_ref[2:3, :] * gcu
    yc = gate_b * conv
    ms = _dot((yc * yc).astype(BF16), gmat_ref[...])
    yc_ref[...] = (yc * lax.rsqrt(ms + EPS) * gmix_ref[...]).astype(BF16)
    k_ref[...] = _dot(xn, w_ref[:, o_q + ATTN_WIDTH:o_q + 2 * ATTN_WIDTH]).astype(BF16)

    zf = zvf[:, ATTN_WIDTH:] + bf_ref[...]
    lf = -(jnp.maximum(-zf, 0.0) + jnp.log1p(jnp.exp(-jnp.abs(zf))))
    lft = lf.T[0:N_HEADS, :]
    hi = lft.astype(BF16).astype(F32)
    r1 = lft - hi
    mid = r1.astype(BF16).astype(F32)
    lo = r1 - mid
    parts = jnp.concatenate([hi, mid, lo], axis=0).astype(BF16)
    r_i = lax.broadcasted_iota(jnp.int32, (tm, tm), 0)
    c_i = lax.broadcasted_iota(jnp.int32, (tm, tm), 1)
    triu = jnp.where(r_i <= c_i, 1.0, 0.0).astype(BF16)
    cs = _dot(parts, triu)
    c = c_carry[...] + ((cs[0:N_HEADS] + cs[N_HEADS:2 * N_HEADS]) + cs[2 * N_HEADS:])
    c_carry[...] = jnp.broadcast_to(c[:, tm - 1:tm], c_carry.shape)
    crow_ref[...] = c


def _in_proj(x, g, w, bf, cw, gmix, gmat):
    B, S, _ = x.shape
    tm = TM_IN
    n_cols = w.shape[1]
    const = lambda b, i: (0, 0)
    single = dict(pipeline_mode=pl.Buffered(1))
    tok = lambda width: pl.BlockSpec((None, tm, width), lambda b, i: (b, i, 0))
    return pl.pallas_call(
        _in_proj_kernel,
        grid=(B, S // tm),
        in_specs=[
            tok(D_MODEL),
            pl.BlockSpec((1, D_MODEL), const, **single),
            pl.BlockSpec((D_MODEL, n_cols), const, **single),
            pl.BlockSpec((1, LANES), const, **single),
            pl.BlockSpec((CONV_K, CONV_WIDTH), const, **single),
            pl.BlockSpec((1, CONV_WIDTH), const, **single),
            pl.BlockSpec((CONV_WIDTH, CONV_WIDTH), const, **single),
        ],
        out_specs=[
            tok(CONV_WIDTH), tok(ATTN_WIDTH), tok(ATTN_WIDTH), tok(ATTN_WIDTH),
            pl.BlockSpec((None, N_HEADS, tm), lambda b, i: (b, 0, i)),
        ],
        out_shape=[
            jax.ShapeDtypeStruct((B, S, CONV_WIDTH), BF16),
            jax.ShapeDtypeStruct((B, S, ATTN_WIDTH), BF16),
            jax.ShapeDtypeStruct((B, S, ATTN_WIDTH), BF16),
            jax.ShapeDtypeStruct((B, S, ATTN_WIDTH), BF16),
            jax.ShapeDtypeStruct((B, N_HEADS, S), F32),
        ],
        scratch_shapes=[
            pltpu.VMEM((SUBLANES, CONV_WIDTH), F32),
            pltpu.VMEM((N_HEADS, TM_IN), F32),
        ],
        compiler_params=pltpu.CompilerParams(
            dimension_semantics=("arbitrary", "arbitrary"),
            vmem_limit_bytes=VMEM_LIMIT_BYTES),
        name="in_proj",
    )(x, g, w, bf, cw, gmix, gmat)


def _attn_kernel(q_ref, k_ref, v_ref, crow_ref, gmix_ref, o_ref,
                 s_scr, acc_scr, mrun_scr, m_scr, q2_scr):
    S = q_ref.shape[0]
    nq = S // TQ
    tiles_per_q = TQ // TK
    pairs = range(PAIRS_PER_STEP)
    lane = lax.broadcasted_iota(jnp.int32, (TQ, LANES), 1)
    head_lanes = [(lane >= hh * GROUP_DIM) & (lane < (hh + 1) * GROUP_DIM)
                  for hh in range(HEADS_PER_BLOCK)]
    ones = jnp.ones((TK, LANES), BF16)

    acc_scr[...] = jnp.zeros_like(acc_scr)
    mrun_scr[...] = jnp.full(mrun_scr.shape, MASK_VALUE, F32)

    def pass2_tile(pp, s0):
        m = m_scr[pp]
        s = s_scr[pp, :, pl.ds(s0, TK)]
        p = jnp.concatenate(
            [jnp.exp(s[:, c0:c0 + LANES] - m) for c0 in range(0, TK, LANES)], axis=1)
        v_aug = jnp.concatenate([v_ref[pl.ds(s0, TK), pp * LANES:(pp + 1) * LANES], ones],
                                axis=1)
        acc_scr[pp] += _dot(p.astype(BF16), v_aug)

    def pass1_tile(pp, s0, crefs, diag_offset=None):
        raw = lax.dot_general(q2_scr[pp], k_ref[pl.ds(s0, TK), pp * LANES:(pp + 1) * LANES],
                              (((1,), (1,)), ((), ())), preferred_element_type=F32)
        parts = []
        for hh in range(HEADS_PER_BLOCK):
            t = raw[hh * TQ:(hh + 1) * TQ] + (crefs[pp][hh]
                                             - crow_ref[pp, hh:hh + 1, pl.ds(s0, TK)])
            if diag_offset is not None:
                r_i = lax.broadcasted_iota(jnp.int32, (TQ, TK), 0)
                c_i = lax.broadcasted_iota(jnp.int32, (TQ, TK), 1)
                t = jnp.where(c_i + diag_offset <= r_i, t, MASK_VALUE)
            parts.append(t)
        t = jnp.concatenate(parts, axis=0)
        s_scr[pp, :, pl.ds(s0, TK)] = t
        tmax = t[:, 0:LANES]
        for c0 in range(LANES, TK, LANES):
            tmax = jnp.maximum(tmax, t[:, c0:c0 + LANES])
        mrun_scr[pp] = jnp.maximum(mrun_scr[pp], tmax)

    def finalize(t0):
        for pp in pairs:
            out = None
            for hh in range(HEADS_PER_BLOCK):
                a = acc_scr[pp, hh * TQ:(hh + 1) * TQ, :]
                o = jnp.where(head_lanes[hh], a[:, 0:LANES] / a[:, LANES:], 0.0)
                ms = jnp.sum(o * o, axis=-1, keepdims=True) * (1.0 / GROUP_DIM)
                o = o * lax.rsqrt(ms + EPS)
                out = o if out is None else out + o
            o_ref[pl.ds(t0, TQ), pp * LANES:(pp + 1) * LANES] = (
                out * gmix_ref[:, pp * LANES:(pp + 1) * LANES]).astype(BF16)
        acc_scr[...] = jnp.zeros_like(acc_scr)

    def q_block(qi, first):
        t0 = pl.multiple_of(qi * TQ, TQ)
        crefs = []
        for pp in pairs:
            q = q_ref[pl.ds(t0, TQ), pp * LANES:(pp + 1) * LANES]
            crefs.append([])
            for hh in range(HEADS_PER_BLOCK):
                q2_scr[pp, hh * TQ:(hh + 1) * TQ, :] = jnp.where(head_lanes[hh], q,
                                                                 jnp.zeros_like(q))
                crefs[pp].append(jnp.min(crow_ref[pp, hh:hh + 1, pl.ds(t0, TQ)],
                                         axis=-1, keepdims=True))

        def both(j, _):
            s0 = pl.multiple_of(j * TK, TK)
            for pp in pairs:
                pass2_tile(pp, s0)
            for pp in pairs:
                pass1_tile(pp, s0, crefs)
            return 0

        if not first:
            lax.fori_loop(0, qi * tiles_per_q, both, 0)
            finalize(pl.multiple_of(t0 - TQ, TQ))

        for d in range(tiles_per_q):
            for pp in pairs:
                pass1_tile(pp, pl.multiple_of(t0 + d * TK, TK), crefs, diag_offset=d * TK)
        for pp in pairs:
            m = jnp.max(mrun_scr[pp], axis=-1, keepdims=True)
            m_scr[pp] = jnp.broadcast_to(m, m_scr.shape[1:])
        mrun_scr[...] = jnp.full(mrun_scr.shape, MASK_VALUE, F32)
        return 0

    q_block(0, True)
    lax.fori_loop(1, nq, lambda qi, _: q_block(qi, False), 0)

    def drain(j, _):
        for d in range(DRAIN_UNROLL):
            for pp in pairs:
                pass2_tile(pp, pl.multiple_of((j * DRAIN_UNROLL + d) * TK, TK))
        return 0

    assert (nq * tiles_per_q) % DRAIN_UNROLL == 0
    lax.fori_loop(0, nq * tiles_per_q // DRAIN_UNROLL, drain, 0)
    finalize(S - TQ)


def _attention(q, k, v, crow, gmix_attn):
    B, S, _ = q.shape
    n_pairs = N_HEADS // HEADS_PER_BLOCK
    rows = HEADS_PER_BLOCK * TQ
    width = PAIRS_PER_STEP * LANES
    seq = pl.BlockSpec((None, S, width), lambda b, j: (b, 0, j))
    return pl.pallas_call(
        _attn_kernel,
        grid=(B, n_pairs // PAIRS_PER_STEP),
        in_specs=[
            seq, seq, seq,
            pl.BlockSpec((None, PAIRS_PER_STEP, HEADS_PER_BLOCK, S), lambda b, j: (b, j, 0, 0)),
            pl.BlockSpec((1, width), lambda b, j: (0, j)),
        ],
        out_specs=seq,
        out_shape=jax.ShapeDtypeStruct((B, S, ATTN_WIDTH), BF16),
        scratch_shapes=[
            pltpu.VMEM((PAIRS_PER_STEP, rows, S), F32),
            pltpu.VMEM((PAIRS_PER_STEP, rows, 2 * LANES), F32),
            pltpu.VMEM((PAIRS_PER_STEP, rows, LANES), F32),
            pltpu.VMEM((PAIRS_PER_STEP, rows, LANES), F32),
            pltpu.VMEM((PAIRS_PER_STEP, rows, LANES), BF16),
        ],
        compiler_params=pltpu.CompilerParams(
            dimension_semantics=("arbitrary", "arbitrary"),
            vmem_limit_bytes=VMEM_LIMIT_BYTES),
        name="forgetting_attention",
    )(q, k, v, crow.reshape(B, n_pairs, HEADS_PER_BLOCK, S), gmix_attn)


def _out_kernel(x_ref, yc_ref, ya_ref, p_ref, wo_ref, gffn_ref, wgu_ref, wd_ref,
                gple_ref, wpg_ref, bpg_ref, wpp_ref, gfin_ref, o_ref):
    y = jnp.concatenate([yc_ref[...], ya_ref[...]], axis=-1)
    h = x_ref[...] + _dot(y, wo_ref[...])
    hn = (h * _rms_scale(h) * gffn_ref[...]).astype(BF16)
    ff = None
    for c0 in range(0, D_FF, FF_CHUNK):
        g = _dot(hn, wgu_ref[:, c0:c0 + FF_CHUNK])
        up = _dot(hn, wgu_ref[:, D_FF + c0:D_FF + c0 + FF_CHUNK])
        a = (g * jax.nn.sigmoid(g) * up).astype(BF16)
        d = _dot(a, wd_ref[c0:c0 + FF_CHUNK, :])
        ff = d if ff is None else ff + d
    h = h + ff
    hn = (h * _rms_scale(h) * gple_ref[...]).astype(BF16)
    gate = jax.nn.sigmoid(_dot(hn, wpg_ref[...]) + bpg_ref[...])
    h = h + gate * _dot(p_ref[...].astype(BF16), wpp_ref[...])
    o_ref[...] = h * _rms_scale(h) * gfin_ref[...]


def _out_block(x, yc, ya, p, wo, gffn, wgu, wd, gple, wpg, bpg, wpp, gfin):
    n_tok = x.shape[0]
    tm = TM_OUT
    const = lambda i: (0, 0)
    single = dict(pipeline_mode=pl.Buffered(1))
    tok = lambda width: pl.BlockSpec((tm, width), lambda i: (i, 0))
    full = lambda a: pl.BlockSpec(a.shape, const, **single)
    return pl.pallas_call(
        _out_kernel,
        grid=(n_tok // tm,),
        in_specs=[tok(D_MODEL), tok(CONV_WIDTH), tok(ATTN_WIDTH), tok(PLE_DIM),
                  full(wo), full(gffn), full(wgu), full(wd), full(gple), full(wpg),
                  full(bpg), full(wpp), full(gfin)],
        out_specs=tok(D_MODEL),
        out_shape=jax.ShapeDtypeStruct((n_tok, D_MODEL), F32),
        compiler_params=pltpu.CompilerParams(
            dimension_semantics=("arbitrary",),
            vmem_limit_bytes=VMEM_LIMIT_BYTES),
        name="out_ffn_ple",
    )(x, yc, ya, p, wo, gffn, wgu, wd, gple, wpg, bpg, wpp, gfin)


def _layer(h, p_i, mix_norm, w_in, b_f, conv_w, mix_out_norm, w_o, ffn_norm, w_gate_up,
           w_down, ple_norm, w_ple_gate, b_ple_gate, w_ple_proj, out_norm):
    B, S, _ = h.shape
    row = lambda a: a.reshape(1, -1).astype(F32)
    n_main = 3 * CONV_WIDTH + 3 * ATTN_WIDTH
    w = jnp.pad(w_in, ((0, 0), (0, LANES - N_HEADS))).astype(BF16)
    bf = jnp.pad(row(b_f), ((0, 0), (0, LANES - N_HEADS)))
    assert w.shape[1] == n_main + LANES
    group = jnp.arange(CONV_WIDTH) // GROUP_DIM
    gmat = jnp.where(group[:, None] == group[None, :], 1.0 / GROUP_DIM, 0.0).astype(BF16)
    gmix = row(mix_out_norm)
    yc, q, k, v, crow = _in_proj(h, row(mix_norm), w, bf, conv_w.astype(F32),
                                 gmix[:, :CONV_WIDTH], gmat)
    ya = _attention(q, k, v, crow, gmix[:, CONV_WIDTH:])
    n_tok = B * S
    out = _out_block(
        h.reshape(n_tok, D_MODEL), yc.reshape(n_tok, CONV_WIDTH),
        ya.reshape(n_tok, ATTN_WIDTH), p_i.reshape(n_tok, PLE_DIM),
        w_o.astype(BF16), row(ffn_norm), w_gate_up.astype(BF16), w_down.astype(BF16),
        row(ple_norm), w_ple_gate.astype(BF16), row(b_ple_gate), w_ple_proj.astype(BF16),
        row(out_norm))
    return out.reshape(B, S, D_MODEL)


def kernel(x, p, mix_norm, w_in, b_f, conv_w, mix_out_norm, w_o, ffn_norm, w_gate_up,
           w_down, ple_norm, w_ple_gate, b_ple_gate, w_ple_proj, final_norm):
    depth = p.shape[0]
    assert depth == 1, "the final RMSNorm is fused into the single layer's last kernel"
    return _layer(x, p[0], mix_norm[0], w_in[0], b_f[0], conv_w[0], mix_out_norm[0],
                  w_o[0], ffn_norm[0], w_gate_up[0], w_down[0], ple_norm[0],
                  w_ple_gate[0], b_ple_gate[0], w_ple_proj[0], final_norm)
```

```python
import jax
import jax.numpy as jnp
from jax import lax
from jax.experimental import pallas as pl
from jax.experimental.pallas import tpu as pltpu

D_MODEL = 1024
PLE_DIM = 256
CONV_WIDTH = 512
ATTN_WIDTH = 512
GROUP_DIM = 64
N_HEADS = ATTN_WIDTH // GROUP_DIM
CONV_K = 3
D_FF = 2816
EPS = 1e-6

LANES = 128
SUBLANES = 8
HEADS_PER_BLOCK = LANES // GROUP_DIM
VMEM_LIMIT_BYTES = 56 * 1024 * 1024

TM_IN = 512
TQ = 512
TK = 512
PAIRS_PER_STEP = 2
TILE_UNROLL = 2
DRAIN_UNROLL = 2
TM_OUT = 512
FF_CHUNK = 256
MASK_VALUE = -1e30

F32 = jnp.float32
BF16 = jnp.bfloat16


def _rms_scale(x):
    return lax.rsqrt(jnp.mean(x * x, axis=-1, keepdims=True) + EPS)


def _dot(a, b):
    return jnp.dot(a, b, preferred_element_type=F32)


def _in_proj_kernel(x_ref, g_ref, w_ref, bf_ref, cw_ref, gmix_ref,
                    yc_ref, q_ref, k_ref, v_ref, crow_ref,
                    conv_carry, c_carry):
    tm = x_ref.shape[0]

    @pl.when(pl.program_id(1) == 0)
    def _():
        conv_carry[...] = jnp.zeros_like(conv_carry)
        c_carry[...] = jnp.zeros_like(c_carry)

    x = x_ref[...]
    xn = (x * _rms_scale(x) * g_ref[...]).astype(BF16)

    cw = CONV_WIDTH
    o_q = 3 * cw
    o_v = o_q + 2 * ATTN_WIDTH
    gate_c = _dot(xn, w_ref[:, cw:2 * cw])
    u = _dot(xn, w_ref[:, 2 * cw:3 * cw])
    zvf = _dot(xn, w_ref[:, o_v:o_v + ATTN_WIDTH + LANES])
    gate_b = _dot(xn, w_ref[:, 0:cw])
    q_ref[...] = (_dot(xn, w_ref[:, o_q:o_q + ATTN_WIDTH]) * (GROUP_DIM ** -0.5)).astype(BF16)
    v_ref[...] = zvf[:, 0:ATTN_WIDTH].astype(BF16)

    gcu = gate_c * u
    carry = conv_carry[...]
    prev1 = pltpu.roll(gcu, 1, axis=0)
    prev2 = pltpu.roll(gcu, 2, axis=0)
    row8 = lax.broadcasted_iota(jnp.int32, (SUBLANES, cw), 0)
    head1 = jnp.where(row8 < 1, pltpu.roll(carry, 1, axis=0), prev1[0:SUBLANES])
    head2 = jnp.where(row8 < 2, pltpu.roll(carry, 2, axis=0), prev2[0:SUBLANES])
    prev1 = jnp.concatenate([head1, prev1[SUBLANES:]], axis=0)
    prev2 = jnp.concatenate([head2, prev2[SUBLANES:]], axis=0)
    conv_carry[...] = gcu[tm - SUBLANES:tm]
    conv = cw_ref[0:1, :] * prev2 + cw_ref[1:2, :] * prev1 + cw_ref[2:3, :] * gcu
    yc = gate_b * conv
    y2 = yc * yc
    lane = lax.broadcasted_iota(jnp.int32, (tm, LANES), 1)
    low = lane < GROUP_DIM
    ms = []
    for c0 in range(0, cw, LANES):
        blk = y2[:, c0:c0 + LANES]
        s_lo = jnp.sum(jnp.where(low, blk, 0.0), axis=-1, keepdims=True)
        s_hi = jnp.sum(jnp.where(low, 0.0, blk), axis=-1, keepdims=True)
        ms.append(jnp.where(low, s_lo, s_hi) * (1.0 / GROUP_DIM))
    ms = jnp.concatenate(ms, axis=1)
    yc_ref[...] = (yc * lax.rsqrt(ms + EPS) * gmix_ref[...]).astype(BF16)
    k_ref[...] = _dot(xn, w_ref[:, o_q + ATTN_WIDTH:o_q + 2 * ATTN_WIDTH]).astype(BF16)

    zf = zvf[:, ATTN_WIDTH:] + bf_ref[...]
    lf = -(jnp.maximum(-zf, 0.0) + jnp.log1p(jnp.exp(-jnp.abs(zf))))
    lft = lf.T[0:N_HEADS, :]
    hi = lft.astype(BF16).astype(F32)
    r1 = lft - hi
    mid = r1.astype(BF16).astype(F32)
    lo = r1 - mid
    parts = jnp.concatenate([hi, mid, lo], axis=0).astype(BF16)
    r_i = lax.broadcasted_iota(jnp.int32, (tm, tm), 0)
    c_i = lax.broadcasted_iota(jnp.int32, (tm, tm), 1)
    triu = jnp.where(r_i <= c_i, 1.0, 0.0).astype(BF16)
    cs = _dot(parts, triu)
    c = c_carry[...] + ((cs[0:N_HEADS] + cs[N_HEADS:2 * N_HEADS]) + cs[2 * N_HEADS:])
    c_carry[...] = jnp.broadcast_to(c[:, tm - 1:tm], c_carry.shape)
    crow_ref[...] = c


def _in_proj(x, g, w, bf, cw, gmix):
    B, S, _ = x.shape
    tm = TM_IN
    n_cols = w.shape[1]
    const = lambda b, i: (0, 0)
    single = dict(pipeline_mode=pl.Buffered(1))
    tok = lambda width: pl.BlockSpec((None, tm, width), lambda b, i: (b, i, 0))
    return pl.pallas_call(
        _in_proj_kernel,
        grid=(B, S // tm),
        in_specs=[
            tok(D_MODEL),
            pl.BlockSpec((1, D_MODEL), const, **single),
            pl.BlockSpec((D_MODEL, n_cols), const, **single),
            pl.BlockSpec((1, LANES), const, **single),
            pl.BlockSpec((CONV_K, CONV_WIDTH), const, **single),
            pl.BlockSpec((1, CONV_WIDTH), const, **single),
        ],
        out_specs=[
            tok(CONV_WIDTH), tok(ATTN_WIDTH), tok(ATTN_WIDTH), tok(ATTN_WIDTH),
            pl.BlockSpec((None, N_HEADS, tm), lambda b, i: (b, 0, i)),
        ],
        out_shape=[
            jax.ShapeDtypeStruct((B, S, CONV_WIDTH), BF16),
            jax.ShapeDtypeStruct((B, S, ATTN_WIDTH), BF16),
            jax.ShapeDtypeStruct((B, S, ATTN_WIDTH), BF16),
            jax.ShapeDtypeStruct((B, S, ATTN_WIDTH), BF16),
            jax.ShapeDtypeStruct((B, N_HEADS, S), F32),
        ],
        scratch_shapes=[
            pltpu.VMEM((SUBLANES, CONV_WIDTH), F32),
            pltpu.VMEM((N_HEADS, TM_IN), F32),
        ],
        compiler_params=pltpu.CompilerParams(
            dimension_semantics=("arbitrary", "arbitrary"),
            vmem_limit_bytes=VMEM_LIMIT_BYTES),
        name="in_proj",
    )(x, g, w, bf, cw, gmix)


def _attn_kernel(q_ref, k_ref, v_ref, crow_ref, gmix_ref, o_ref,
                 s_scr, acc_scr, mrun_scr, m_scr, q2_scr):
    S = q_ref.shape[0]
    nq = S // TQ
    tiles_per_q = TQ // TK
    pairs = range(PAIRS_PER_STEP)
    head0 = pl.program_id(1) * (PAIRS_PER_STEP * HEADS_PER_BLOCK)
    lane = lax.broadcasted_iota(jnp.int32, (TQ, LANES), 1)
    head_lanes = [(lane >= hh * GROUP_DIM) & (lane < (hh + 1) * GROUP_DIM)
                  for hh in range(HEADS_PER_BLOCK)]
    ones = jnp.ones((TK, LANES), BF16)

    acc_scr[...] = jnp.zeros_like(acc_scr)
    mrun_scr[...] = jnp.full(mrun_scr.shape, MASK_VALUE, F32)

    def pass2_tile(pp, s0):
        m = m_scr[pp]
        s = s_scr[pp, :, pl.ds(s0, TK)]
        p = jnp.concatenate(
            [jnp.exp(s[:, c0:c0 + LANES] - m) for c0 in range(0, TK, LANES)], axis=1)
        v_aug = jnp.concatenate([v_ref[pl.ds(s0, TK), pp * LANES:(pp + 1) * LANES], ones],
                                axis=1)
        acc_scr[pp] += _dot(p.astype(BF16), v_aug)

    def pass1_tile(pp, s0, crefs, diag_offset=None):
        raw = lax.dot_general(q2_scr[pp], k_ref[pl.ds(s0, TK), pp * LANES:(pp + 1) * LANES],
                              (((1,), (1,)), ((), ())), preferred_element_type=F32)
        parts = []
        for hh in range(HEADS_PER_BLOCK):
            t = raw[hh * TQ:(hh + 1) * TQ] + (crefs[pp][hh]
                                             - crow_ref[pl.ds(head0 + pp * HEADS_PER_BLOCK + hh, 1),
                                                        pl.ds(s0, TK)])
            if diag_offset is not None:
                r_i = lax.broadcasted_iota(jnp.int32, (TQ, TK), 0)
                c_i = lax.broadcasted_iota(jnp.int32, (TQ, TK), 1)
                t = jnp.where(c_i + diag_offset <= r_i, t, MASK_VALUE)
            parts.append(t)
        t = jnp.concatenate(parts, axis=0)
        s_scr[pp, :, pl.ds(s0, TK)] = t
        tmax = t[:, 0:LANES]
        for c0 in range(LANES, TK, LANES):
            tmax = jnp.maximum(tmax, t[:, c0:c0 + LANES])
        mrun_scr[pp] = jnp.maximum(mrun_scr[pp], tmax)

    def finalize(t0):
        for pp in pairs:
            out = None
            for hh in range(HEADS_PER_BLOCK):
                a = acc_scr[pp, hh * TQ:(hh + 1) * TQ, :]
                o = jnp.where(head_lanes[hh], a[:, 0:LANES] / a[:, LANES:], 0.0)
                ms = jnp.sum(o * o, axis=-1, keepdims=True) * (1.0 / GROUP_DIM)
                o = o * lax.rsqrt(ms + EPS)
                out = o if out is None else out + o
            o_ref[pl.ds(t0, TQ), pp * LANES:(pp + 1) * LANES] = (
                out * gmix_ref[:, pp * LANES:(pp + 1) * LANES]).astype(BF16)
        acc_scr[...] = jnp.zeros_like(acc_scr)

    def q_block(qi, first):
        t0 = pl.multiple_of(qi * TQ, TQ)
        crefs = []
        for pp in pairs:
            q = q_ref[pl.ds(t0, TQ), pp * LANES:(pp + 1) * LANES]
            crefs.append([])
            for hh in range(HEADS_PER_BLOCK):
                q2_scr[pp, hh * TQ:(hh + 1) * TQ, :] = jnp.where(head_lanes[hh], q,
                                                                 jnp.zeros_like(q))
                crefs[pp].append(jnp.min(
                    crow_ref[pl.ds(head0 + pp * HEADS_PER_BLOCK + hh, 1), pl.ds(t0, TQ)],
                    axis=-1, keepdims=True))

        def both(j, n_tiles):
            starts = [pl.multiple_of((j * n_tiles + d) * TK, TK) for d in range(n_tiles)]
            for s0 in starts:
                for pp in pairs:
                    pass2_tile(pp, s0)
            for s0 in starts:
                for pp in pairs:
                    pass1_tile(pp, s0, crefs)
            return 0

        if not first:
            n_full = qi * tiles_per_q
            n_steps = n_full // TILE_UNROLL
            lax.fori_loop(0, n_steps, lambda j, _: both(j, TILE_UNROLL), 0)
            lax.fori_loop(n_steps * TILE_UNROLL, n_full, lambda j, _: both(j, 1), 0)
            finalize(pl.multiple_of(t0 - TQ, TQ))

        for d in range(tiles_per_q):
            for pp in pairs:
                pass1_tile(pp, pl.multiple_of(t0 + d * TK, TK), crefs, diag_offset=d * TK)
        for pp in pairs:
            m = jnp.max(mrun_scr[pp], axis=-1, keepdims=True)
            m_scr[pp] = jnp.broadcast_to(m, m_scr.shape[1:])
        mrun_scr[...] = jnp.full(mrun_scr.shape, MASK_VALUE, F32)
        return 0

    q_block(0, True)
    lax.fori_loop(1, nq, lambda qi, _: q_block(qi, False), 0)

    def drain(j, _):
        for d in range(DRAIN_UNROLL):
            for pp in pairs:
                pass2_tile(pp, pl.multiple_of((j * DRAIN_UNROLL + d) * TK, TK))
        return 0

    assert (nq * tiles_per_q) % DRAIN_UNROLL == 0
    lax.fori_loop(0, nq * tiles_per_q // DRAIN_UNROLL, drain, 0)
    finalize(S - TQ)


def _attention(q, k, v, crow, gmix_attn):
    B, S, _ = q.shape
    n_pairs = N_HEADS // HEADS_PER_BLOCK
    rows = HEADS_PER_BLOCK * TQ
    width = PAIRS_PER_STEP * LANES
    seq = pl.BlockSpec((None, S, width), lambda b, j: (b, 0, j))
    return pl.pallas_call(
        _attn_kernel,
        grid=(B, n_pairs // PAIRS_PER_STEP),
        in_specs=[
            seq, seq, seq,
            pl.BlockSpec((None, N_HEADS, S), lambda b, j: (b, 0, 0)),
            pl.BlockSpec((1, width), lambda b, j: (0, j)),
        ],
        out_specs=seq,
        out_shape=jax.ShapeDtypeStruct((B, S, ATTN_WIDTH), BF16),
        scratch_shapes=[
            pltpu.VMEM((PAIRS_PER_STEP, rows, S), F32),
            pltpu.VMEM((PAIRS_PER_STEP, rows, 2 * LANES), F32),
            pltpu.VMEM((PAIRS_PER_STEP, rows, LANES), F32),
            pltpu.VMEM((PAIRS_PER_STEP, rows, LANES), F32),
            pltpu.VMEM((PAIRS_PER_STEP, rows, LANES), BF16),
        ],
        compiler_params=pltpu.CompilerParams(
            dimension_semantics=("arbitrary", "arbitrary"),
            vmem_limit_bytes=VMEM_LIMIT_BYTES),
        name="forgetting_attention",
    )(q, k, v, crow, gmix_attn)


def _out_kernel(x_ref, yc_ref, ya_ref, p_ref, wo_ref, gffn_ref, wgu_ref, wd_ref,
                gple_ref, wpg_ref, bpg_ref, wpp_ref, gfin_ref, o_ref):
    y = jnp.concatenate([yc_ref[...], ya_ref[...]], axis=-1)
    h = x_ref[...] + _dot(y, wo_ref[...])
    hn = (h * _rms_scale(h) * gffn_ref[...]).astype(BF16)
    ff = None
    for c0 in range(0, D_FF, FF_CHUNK):
        g = _dot(hn, wgu_ref[:, c0:c0 + FF_CHUNK])
        up = _dot(hn, wgu_ref[:, D_FF + c0:D_FF + c0 + FF_CHUNK])
        a = (g * jax.nn.sigmoid(g) * up).astype(BF16)
        d = _dot(a, wd_ref[c0:c0 + FF_CHUNK, :])
        ff = d if ff is None else ff + d
    h = h + ff
    hn = (h * _rms_scale(h) * gple_ref[...]).astype(BF16)
    gate = jax.nn.sigmoid(_dot(hn, wpg_ref[...]) + bpg_ref[...])
    h = h + gate * _dot(p_ref[...].astype(BF16), wpp_ref[...])
    o_ref[...] = h * _rms_scale(h) * gfin_ref[...]


def _out_block(x, yc, ya, p, wo, gffn, wgu, wd, gple, wpg, bpg, wpp, gfin):
    n_tok = x.shape[0]
    tm = TM_OUT
    const = lambda i: (0, 0)
    single = dict(pipeline_mode=pl.Buffered(1))
    tok = lambda width: pl.BlockSpec((tm, width), lambda i: (i, 0))
    full = lambda a: pl.BlockSpec(a.shape, const, **single)
    return pl.pallas_call(
        _out_kernel,
        grid=(n_tok // tm,),
        in_specs=[tok(D_MODEL), tok(CONV_WIDTH), tok(ATTN_WIDTH), tok(PLE_DIM),
                  full(wo), full(gffn), full(wgu), full(wd), full(gple), full(wpg),
                  full(bpg), full(wpp), full(gfin)],
        out_specs=tok(D_MODEL),
        out_shape=jax.ShapeDtypeStruct((n_tok, D_MODEL), F32),
        compiler_params=pltpu.CompilerParams(
            dimension_semantics=("arbitrary",),
            vmem_limit_bytes=VMEM_LIMIT_BYTES),
        name="out_ffn_ple",
    )(x, yc, ya, p, wo, gffn, wgu, wd, gple, wpg, bpg, wpp, gfin)


def _layer(h, p_i, mix_norm, w_in, b_f, conv_w, mix_out_norm, w_o, ffn_norm, w_gate_up,
           w_down, ple_norm, w_ple_gate, b_ple_gate, w_ple_proj, out_norm):
    B, S, _ = h.shape
    row = lambda a: a.reshape(1, -1).astype(F32)
    n_main = 3 * CONV_WIDTH + 3 * ATTN_WIDTH
    w = jnp.pad(w_in, ((0, 0), (0, LANES - N_HEADS))).astype(BF16)
    bf = jnp.pad(row(b_f), ((0, 0), (0, LANES - N_HEADS)))
    assert w.shape[1] == n_main + LANES
    gmix = row(mix_out_norm)
    yc, q, k, v, crow = _in_proj(h, row(mix_norm), w, bf, conv_w.astype(F32),
                                 gmix[:, :CONV_WIDTH])
    ya = _attention(q, k, v, crow, gmix[:, CONV_WIDTH:])
    n_tok = B * S
    out = _out_block(
        h.reshape(n_tok, D_MODEL), yc.reshape(n_tok, CONV_WIDTH),
        ya.reshape(n_tok, ATTN_WIDTH), p_i.reshape(n_tok, PLE_DIM),
        w_o.astype(BF16), row(ffn_norm), w_gate_up.astype(BF16), w_down.astype(BF16),
        row(ple_norm), w_ple_gate.astype(BF16), row(b_ple_gate), w_ple_proj.astype(BF16),
        row(out_norm))
    return out.reshape(B, S, D_MODEL)


def kernel(x, p, mix_norm, w_in, b_f, conv_w, mix_out_norm, w_o, ffn_norm, w_gate_up,
           w_down, ple_norm, w_ple_gate, b_ple_gate, w_ple_proj, final_norm):
    depth = p.shape[0]
    assert depth == 1, "the final RMSNorm is fused into the single layer's last kernel"
    return _layer(x, p[0], mix_norm[0], w_in[0], b_f[0], conv_w[0], mix_out_norm[0],
                  w_o[0], ffn_norm[0], w_gate_up[0], w_down[0], ple_norm[0],
                  w_ple_gate[0], b_ple_gate[0], w_ple_proj[0], final_norm)
```

```python
import jax
import jax.numpy as jnp
from jax import lax
from jax.experimental import pallas as pl
from jax.experimental.pallas import tpu as pltpu

D_MODEL = 1024
PLE_DIM = 256
CONV_WIDTH = 512
ATTN_WIDTH = 512
GROUP_DIM = 64
N_HEADS = ATTN_WIDTH // GROUP_DIM
CONV_K = 3
D_FF = 2816
EPS = 1e-6

LANES = 128
SUBLANES = 8
HEADS_PER_BLOCK = LANES // GROUP_DIM
VMEM_LIMIT_BYTES = 56 * 1024 * 1024

TM_IN = 512
TQ = 512
TK = 512
PAIRS_PER_STEP = 2
TILE_UNROLL = 2
DRAIN_UNROLL = 2
TM_OUT = 512
FF_CHUNK = 256
MASK_VALUE = -1e30

F32 = jnp.float32
BF16 = jnp.bfloat16


def _rms_scale(x):
    return lax.rsqrt(jnp.mean(x * x, axis=-1, keepdims=True) + EPS)


def _dot(a, b):
    return jnp.dot(a, b, preferred_element_type=F32)


def _in_proj_kernel(x_ref, g_ref, w_ref, bf_ref, cw_ref, gmix_ref,
                    yc_ref, q_ref, k_ref, v_ref, crow_ref,
                    conv_carry, c_carry):
    tm = x_ref.shape[0]

    @pl.when(pl.program_id(1) == 0)
    def _():
        conv_carry[...] = jnp.zeros_like(conv_carry)
        c_carry[...] = jnp.zeros_like(c_carry)

    x = x_ref[...]
    xn = (x * _rms_scale(x) * g_ref[...]).astype(BF16)

    cw = CONV_WIDTH
    o_q = 3 * cw
    o_v = o_q + 2 * ATTN_WIDTH
    gate_c = _dot(xn, w_ref[:, cw:2 * cw])
    u = _dot(xn, w_ref[:, 2 * cw:3 * cw])
    zvf = _dot(xn, w_ref[:, o_v:o_v + ATTN_WIDTH + LANES])
    gate_b = _dot(xn, w_ref[:, 0:cw])
    q_ref[...] = (_dot(xn, w_ref[:, o_q:o_q + ATTN_WIDTH]) * (GROUP_DIM ** -0.5)).astype(BF16)
    v_ref[...] = zvf[:, 0:ATTN_WIDTH].astype(BF16)

    gcu = gate_c * u
    carry = conv_carry[...]
    prev1 = pltpu.roll(gcu, 1, axis=0)
    prev2 = pltpu.roll(gcu, 2, axis=0)
    row8 = lax.broadcasted_iota(jnp.int32, (SUBLANES, cw), 0)
    head1 = jnp.where(row8 < 1, pltpu.roll(carry, 1, axis=0), prev1[0:SUBLANES])
    head2 = jnp.where(row8 < 2, pltpu.roll(carry, 2, axis=0), prev2[0:SUBLANES])
    prev1 = jnp.concatenate([head1, prev1[SUBLANES:]], axis=0)
    prev2 = jnp.concatenate([head2, prev2[SUBLANES:]], axis=0)
    conv_carry[...] = gcu[tm - SUBLANES:tm]
    conv = cw_ref[0:1, :] * prev2 + cw_ref[1:2, :] * prev1 + cw_ref[2:3, :] * gcu
    yc = gate_b * conv
    y2 = yc * yc
    lane = lax.broadcasted_iota(jnp.int32, (tm, LANES), 1)
    low = lane < GROUP_DIM
    ms = []
    for c0 in range(0, cw, LANES):
        blk = y2[:, c0:c0 + LANES]
        s_lo = jnp.sum(jnp.where(low, blk, 0.0), axis=-1, keepdims=True)
        s_hi = jnp.sum(jnp.where(low, 0.0, blk), axis=-1, keepdims=True)
        ms.append(jnp.where(low, s_lo, s_hi) * (1.0 / GROUP_DIM))
    ms = jnp.concatenate(ms, axis=1)
    yc_ref[...] = (yc * lax.rsqrt(ms + EPS) * gmix_ref[...]).astype(BF16)
    k_ref[...] = _dot(xn, w_ref[:, o_q + ATTN_WIDTH:o_q + 2 * ATTN_WIDTH]).astype(BF16)

    zf = zvf[:, ATTN_WIDTH:] + bf_ref[...]
    lf = -(jnp.maximum(-zf, 0.0) + jnp.log1p(jnp.exp(-jnp.abs(zf))))
    lft = lf.T[0:N_HEADS, :]
    hi = lft.astype(BF16).astype(F32)
    r1 = lft - hi
    mid = r1.astype(BF16).astype(F32)
    lo = r1 - mid
    parts = jnp.concatenate([hi, mid, lo], axis=0).astype(BF16)
    r_i = lax.broadcasted_iota(jnp.int32, (tm, tm), 0)
    c_i = lax.broadcasted_iota(jnp.int32, (tm, tm), 1)
    triu = jnp.where(r_i <= c_i, 1.0, 0.0).astype(BF16)
    cs = _dot(parts, triu)
    c = c_carry[...] + ((cs[0:N_HEADS] + cs[N_HEADS:2 * N_HEADS]) + cs[2 * N_HEADS:])
    c_carry[...] = jnp.broadcast_to(c[:, tm - 1:tm], c_carry.shape)
    crow_ref[...] = c


def _in_proj(x, g, w, bf, cw, gmix):
    B, S, _ = x.shape
    tm = TM_IN
    n_cols = w.shape[1]
    const = lambda b, i: (0, 0)
    single = dict(pipeline_mode=pl.Buffered(1))
    tok = lambda width: pl.BlockSpec((None, tm, width), lambda b, i: (b, i, 0))
    return pl.pallas_call(
        _in_proj_kernel,
        grid=(B, S // tm),
        in_specs=[
            tok(D_MODEL),
            pl.BlockSpec((1, D_MODEL), const, **single),
            pl.BlockSpec((D_MODEL, n_cols), const, **single),
            pl.BlockSpec((1, LANES), const, **single),
            pl.BlockSpec((CONV_K, CONV_WIDTH), const, **single),
            pl.BlockSpec((1, CONV_WIDTH), const, **single),
        ],
        out_specs=[
            tok(CONV_WIDTH), tok(ATTN_WIDTH), tok(ATTN_WIDTH), tok(ATTN_WIDTH),
            pl.BlockSpec((None, N_HEADS, tm), lambda b, i: (b, 0, i)),
        ],
        out_shape=[
            jax.ShapeDtypeStruct((B, S, CONV_WIDTH), BF16),
            jax.ShapeDtypeStruct((B, S, ATTN_WIDTH), BF16),
            jax.ShapeDtypeStruct((B, S, ATTN_WIDTH), BF16),
            jax.ShapeDtypeStruct((B, S, ATTN_WIDTH), BF16),
            jax.ShapeDtypeStruct((B, N_HEADS, S), F32),
        ],
        scratch_shapes=[
            pltpu.VMEM((SUBLANES, CONV_WIDTH), F32),
            pltpu.VMEM((N_HEADS, TM_IN), F32),
        ],
        compiler_params=pltpu.CompilerParams(
            dimension_semantics=("arbitrary", "arbitrary"),
            vmem_limit_bytes=VMEM_LIMIT_BYTES),
        name="in_proj",
    )(x, g, w, bf, cw, gmix)


def _attn_kernel(q_ref, k_ref, v_ref, crow_ref, gmix_ref, o_ref,
                 s_scr, acc_scr, mrun_scr, m_scr, q2_scr):
    assert TQ == TK, "one key tile per query block sits on the diagonal"
    S = q_ref.shape[0]
    nq = S // TQ
    half = TQ // 2
    pairs = range(PAIRS_PER_STEP)
    heads = range(HEADS_PER_BLOCK)
    head0 = pl.program_id(1) * (PAIRS_PER_STEP * HEADS_PER_BLOCK)
    lane = lax.broadcasted_iota(jnp.int32, (TQ, LANES), 1)
    head_lanes = [(lane >= hh * GROUP_DIM) & (lane < (hh + 1) * GROUP_DIM) for hh in heads]
    ones = jnp.ones((TK, LANES), BF16)
    top_rows = [slice(hh * TQ, hh * TQ + half) for hh in heads]
    bot_rows = [slice(hh * TQ + half, (hh + 1) * TQ) for hh in heads]

    acc_scr[...] = jnp.zeros_like(acc_scr)
    mrun_scr[...] = jnp.full(mrun_scr.shape, MASK_VALUE, F32)

    def crow(pp, hh, start, size):
        return crow_ref[pl.ds(head0 + pp * HEADS_PER_BLOCK + hh, 1), pl.ds(start, size)]

    def lane_block_max(t):
        tmax = t[:, 0:LANES]
        for c0 in range(LANES, t.shape[1], LANES):
            tmax = jnp.maximum(tmax, t[:, c0:c0 + LANES])
        return tmax

    def exp_pv(pp, row_slices, s0, n_keys):
        m = jnp.concatenate([m_scr[pp, r, :] for r in row_slices], axis=0)
        s = jnp.concatenate([s_scr[pp, r, pl.ds(s0, n_keys)] for r in row_slices], axis=0)
        p = jnp.concatenate(
            [jnp.exp(s[:, c0:c0 + LANES] - m) for c0 in range(0, n_keys, LANES)], axis=1)
        v_aug = jnp.concatenate(
            [v_ref[pl.ds(s0, n_keys), pp * LANES:(pp + 1) * LANES], ones[0:n_keys]], axis=1)
        pv = _dot(p.astype(BF16), v_aug)
        r0 = 0
        for r in row_slices:
            n = r.stop - r.start
            acc_scr[pp, r, :] += pv[r0:r0 + n]
            r0 += n

    def pass2_tile(pp, s0):
        exp_pv(pp, [slice(0, HEADS_PER_BLOCK * TQ)], s0, TK)

    def pass2_diag(pp, s0):
        exp_pv(pp, top_rows, s0, half)
        exp_pv(pp, bot_rows, s0, TK)

    def pass1_tile(pp, s0, crefs):
        raw = lax.dot_general(q2_scr[pp], k_ref[pl.ds(s0, TK), pp * LANES:(pp + 1) * LANES],
                              (((1,), (1,)), ((), ())), preferred_element_type=F32)
        parts = [raw[hh * TQ:(hh + 1) * TQ] + (crefs[pp][hh] - crow(pp, hh, s0, TK))
                 for hh in heads]
        t = jnp.concatenate(parts, axis=0)
        s_scr[pp, :, pl.ds(s0, TK)] = t
        mrun_scr[pp] = jnp.maximum(mrun_scr[pp], lane_block_max(t))

    def pass1_diag(pp, t0, crefs):
        nt = (((1,), (1,)), ((), ()))
        t1 = pl.multiple_of(t0 + half, half)
        k_lo = k_ref[pl.ds(t0, half), pp * LANES:(pp + 1) * LANES]
        k_hi = k_ref[pl.ds(t1, half), pp * LANES:(pp + 1) * LANES]
        raw_lo = lax.dot_general(q2_scr[pp], k_lo, nt, preferred_element_type=F32)
        q_bot = jnp.concatenate([q2_scr[pp, r, :] for r in bot_rows], axis=0)
        raw_hi = lax.dot_general(q_bot, k_hi, nt, preferred_element_type=F32)
        causal = (lax.broadcasted_iota(jnp.int32, (half, half), 1)
                  <= lax.broadcasted_iota(jnp.int32, (half, half), 0))
        for hh in heads:
            bias_lo = crefs[pp][hh] - crow(pp, hh, t0, half)
            bias_hi = crefs[pp][hh] - crow(pp, hh, t1, half)
            top = jnp.where(causal, raw_lo[top_rows[hh]] + bias_lo, MASK_VALUE)
            bot_lo = raw_lo[bot_rows[hh]] + bias_lo
            bot_hi = jnp.where(causal, raw_hi[hh * half:(hh + 1) * half] + bias_hi, MASK_VALUE)
            s_scr[pp, top_rows[hh], pl.ds(t0, half)] = top
            s_scr[pp, bot_rows[hh], pl.ds(t0, half)] = bot_lo
            s_scr[pp, bot_rows[hh], pl.ds(t1, half)] = bot_hi
            mrun_scr[pp, top_rows[hh], :] = jnp.maximum(mrun_scr[pp, top_rows[hh], :],
                                                        lane_block_max(top))
            mrun_scr[pp, bot_rows[hh], :] = jnp.maximum(
                mrun_scr[pp, bot_rows[hh], :],
                jnp.maximum(lane_block_max(bot_lo), lane_block_max(bot_hi)))

    def finalize(t0):
        for pp in pairs:
            out = None
            for hh in heads:
                a = acc_scr[pp, hh * TQ:(hh + 1) * TQ, :]
                o = jnp.where(head_lanes[hh], a[:, 0:LANES] / a[:, LANES:], 0.0)
                ms = jnp.sum(o * o, axis=-1, keepdims=True) * (1.0 / GROUP_DIM)
                o = o * lax.rsqrt(ms + EPS)
                out = o if out is None else out + o
            o_ref[pl.ds(t0, TQ), pp * LANES:(pp + 1) * LANES] = (
                out * gmix_ref[:, pp * LANES:(pp + 1) * LANES]).astype(BF16)
        acc_scr[...] = jnp.zeros_like(acc_scr)

    def q_block(qi, first):
        t0 = pl.multiple_of(qi * TQ, TQ)
        crefs = []
        for pp in pairs:
            q = q_ref[pl.ds(t0, TQ), pp * LANES:(pp + 1) * LANES]
            crefs.append([])
            for hh in heads:
                q2_scr[pp, hh * TQ:(hh + 1) * TQ, :] = jnp.where(head_lanes[hh], q,
                                                                 jnp.zeros_like(q))
                crefs[pp].append(jnp.min(crow(pp, hh, t0, TQ), axis=-1, keepdims=True))

        def both(j, n_tiles):
            starts = [pl.multiple_of((j * n_tiles + d) * TK, TK) for d in range(n_tiles)]
            for s0 in starts:
                for pp in pairs:
                    pass2_tile(pp, s0)
            for s0 in starts:
                for pp in pairs:
                    pass1_tile(pp, s0, crefs)
            return 0

        if not first:
            n_full = qi - 1
            n_steps = n_full // TILE_UNROLL
            lax.fori_loop(0, n_steps, lambda j, _: both(j, TILE_UNROLL), 0)
            lax.fori_loop(n_steps * TILE_UNROLL, n_full, lambda j, _: both(j, 1), 0)
            s_prev = pl.multiple_of(t0 - TQ, TQ)
            for pp in pairs:
                pass2_diag(pp, s_prev)
            for pp in pairs:
                pass1_tile(pp, s_prev, crefs)
            finalize(s_prev)

        for pp in pairs:
            pass1_diag(pp, t0, crefs)
        for pp in pairs:
            m = jnp.max(mrun_scr[pp], axis=-1, keepdims=True)
            m_scr[pp] = jnp.broadcast_to(m, m_scr.shape[1:])
        mrun_scr[...] = jnp.full(mrun_scr.shape, MASK_VALUE, F32)
        return 0

    q_block(0, True)
    lax.fori_loop(1, nq, lambda qi, _: q_block(qi, False), 0)

    def drain(j, _):
        for d in range(DRAIN_UNROLL):
            for pp in pairs:
                pass2_tile(pp, pl.multiple_of((j * DRAIN_UNROLL + d) * TK, TK))
        return 0

    n_steps = (nq - 1) // DRAIN_UNROLL
    lax.fori_loop(0, n_steps, drain, 0)
    for j in range(n_steps * DRAIN_UNROLL, nq - 1):
        for pp in pairs:
            pass2_tile(pp, j * TK)
    for pp in pairs:
        pass2_diag(pp, S - TQ)
    finalize(S - TQ)


def _attention(q, k, v, crow, gmix_attn):
    B, S, _ = q.shape
    n_pairs = N_HEADS // HEADS_PER_BLOCK
    rows = HEADS_PER_BLOCK * TQ
    width = PAIRS_PER_STEP * LANES
    seq = pl.BlockSpec((None, S, width), lambda b, j: (b, 0, j))
    return pl.pallas_call(
        _attn_kernel,
        grid=(B, n_pairs // PAIRS_PER_STEP),
        in_specs=[
            seq, seq, seq,
            pl.BlockSpec((None, N_HEADS, S), lambda b, j: (b, 0, 0)),
            pl.BlockSpec((1, width), lambda b, j: (0, j)),
        ],
        out_specs=seq,
        out_shape=jax.ShapeDtypeStruct((B, S, ATTN_WIDTH), BF16),
        scratch_shapes=[
            pltpu.VMEM((PAIRS_PER_STEP, rows, S), F32),
            pltpu.VMEM((PAIRS_PER_STEP, rows, 2 * LANES), F32),
            pltpu.VMEM((PAIRS_PER_STEP, rows, LANES), F32),
            pltpu.VMEM((PAIRS_PER_STEP, rows, LANES), F32),
            pltpu.VMEM((PAIRS_PER_STEP, rows, LANES), BF16),
        ],
        compiler_params=pltpu.CompilerParams(
            dimension_semantics=("arbitrary", "arbitrary"),
            vmem_limit_bytes=VMEM_LIMIT_BYTES),
        name="forgetting_attention",
    )(q, k, v, crow, gmix_attn)


def _out_kernel(x_ref, yc_ref, ya_ref, p_ref, wo_ref, gffn_ref, wgu_ref, wd_ref,
                gple_ref, wpg_ref, bpg_ref, wpp_ref, gfin_ref, o_ref):
    y = jnp.concatenate([yc_ref[...], ya_ref[...]], axis=-1)
    h = x_ref[...] + _dot(y, wo_ref[...])
    hn = (h * _rms_scale(h) * gffn_ref[...]).astype(BF16)
    ff = None
    for c0 in range(0, D_FF, FF_CHUNK):
        g = _dot(hn, wgu_ref[:, c0:c0 + FF_CHUNK])
        up = _dot(hn, wgu_ref[:, D_FF + c0:D_FF + c0 + FF_CHUNK])
        a = (g * jax.nn.sigmoid(g) * up).astype(BF16)
        d = _dot(a, wd_ref[c0:c0 + FF_CHUNK, :])
        ff = d if ff is None else ff + d
    h = h + ff
    hn = (h * _rms_scale(h) * gple_ref[...]).astype(BF16)
    gate = jax.nn.sigmoid(_dot(hn, wpg_ref[...]) + bpg_ref[...])
    h = h + gate * _dot(p_ref[...].astype(BF16), wpp_ref[...])
    o_ref[...] = h * _rms_scale(h) * gfin_ref[...]


def _out_block(x, yc, ya, p, wo, gffn, wgu, wd, gple, wpg, bpg, wpp, gfin):
    n_tok = x.shape[0]
    tm = TM_OUT
    const = lambda i: (0, 0)
    single = dict(pipeline_mode=pl.Buffered(1))
    tok = lambda width: pl.BlockSpec((tm, width), lambda i: (i, 0))
    full = lambda a: pl.BlockSpec(a.shape, const, **single)
    return pl.pallas_call(
        _out_kernel,
        grid=(n_tok // tm,),
        in_specs=[tok(D_MODEL), tok(CONV_WIDTH), tok(ATTN_WIDTH), tok(PLE_DIM),
                  full(wo), full(gffn), full(wgu), full(wd), full(gple), full(wpg),
                  full(bpg), full(wpp), full(gfin)],
        out_specs=tok(D_MODEL),
        out_shape=jax.ShapeDtypeStruct((n_tok, D_MODEL), F32),
        compiler_params=pltpu.CompilerParams(
            dimension_semantics=("arbitrary",),
            vmem_limit_bytes=VMEM_LIMIT_BYTES),
        name="out_ffn_ple",
    )(x, yc, ya, p, wo, gffn, wgu, wd, gple, wpg, bpg, wpp, gfin)


def _layer(h, p_i, mix_norm, w_in, b_f, conv_w, mix_out_norm, w_o, ffn_norm, w_gate_up,
           w_down, ple_norm, w_ple_gate, b_ple_gate, w_ple_proj, out_norm):
    B, S, _ = h.shape
    row = lambda a: a.reshape(1, -1).astype(F32)
    n_main = 3 * CONV_WIDTH + 3 * ATTN_WIDTH
    w = jnp.pad(w_in, ((0, 0), (0, LANES - N_HEADS))).astype(BF16)
    bf = jnp.pad(row(b_f), ((0, 0), (0, LANES - N_HEADS)))
    assert w.shape[1] == n_main + LANES
    gmix = row(mix_out_norm)
    yc, q, k, v, crow = _in_proj(h, row(mix_norm), w, bf, conv_w.astype(F32),
                                 gmix[:, :CONV_WIDTH])
    ya = _attention(q, k, v, crow, gmix[:, CONV_WIDTH:])
    n_tok = B * S
    out = _out_block(
        h.reshape(n_tok, D_MODEL), yc.reshape(n_tok, CONV_WIDTH),
        ya.reshape(n_tok, ATTN_WIDTH), p_i.reshape(n_tok, PLE_DIM),
        w_o.astype(BF16), row(ffn_norm), w_gate_up.astype(BF16), w_down.astype(BF16),
        row(ple_norm), w_ple_gate.astype(BF16), row(b_ple_gate), w_ple_proj.astype(BF16),
        row(out_norm))
    return out.reshape(B, S, D_MODEL)


def kernel(x, p, mix_norm, w_in, b_f, conv_w, mix_out_norm, w_o, ffn_norm, w_gate_up,
           w_down, ple_norm, w_ple_gate, b_ple_gate, w_ple_proj, final_norm):
    depth = p.shape[0]
    assert depth == 1, "the final RMSNorm is fused into the single layer's last kernel"
    return _layer(x, p[0], mix_norm[0], w_in[0], b_f[0], conv_w[0], mix_out_norm[0],
                  w_o[0], ffn_norm[0], w_gate_up[0], w_down[0], ple_norm[0],
                  w_ple_gate[0], b_ple_gate[0], w_ple_proj[0], final_norm)
```

```python
import jax
import jax.numpy as jnp
from jax import lax
from jax.experimental import pallas as pl
from jax.experimental.pallas import tpu as pltpu

D_MODEL = 1024
PLE_DIM = 256
CONV_WIDTH = 512
ATTN_WIDTH = 512
GROUP_DIM = 64
N_HEADS = ATTN_WIDTH // GROUP_DIM
CONV_K = 3
D_FF = 2816
EPS = 1e-6

LANES = 128
SUBLANES = 8
HEADS_PER_BLOCK = LANES // GROUP_DIM
VMEM_LIMIT_BYTES = 56 * 1024 * 1024

TM_IN = 512
TQ = 512
TK = 512
PAIRS_PER_STEP = 2
TILE_UNROLL = 2
DRAIN_UNROLL = 2
TM_OUT = 512
FF_CHUNK = 256
MASK_VALUE = -1e30

F32 = jnp.float32
BF16 = jnp.bfloat16


def _rms_scale(x):
    return lax.rsqrt(jnp.mean(x * x, axis=-1, keepdims=True) + EPS)


def _dot(a, b):
    return jnp.dot(a, b, preferred_element_type=F32)


def _in_proj_kernel(x_ref, g_ref, w_ref, bf_ref, cw_ref, gmix_ref,
                    yc_ref, q_ref, k_ref, v_ref, crow_ref,
                    conv_carry, c_carry):
    tm = x_ref.shape[0]

    @pl.when(pl.program_id(1) == 0)
    def _():
        conv_carry[...] = jnp.zeros_like(conv_carry)
        c_carry[...] = jnp.zeros_like(c_carry)

    x = x_ref[...]
    xn = (x * _rms_scale(x) * g_ref[...]).astype(BF16)

    cw = CONV_WIDTH
    o_q = 3 * cw
    o_v = o_q + 2 * ATTN_WIDTH
    gate_c = _dot(xn, w_ref[:, cw:2 * cw])
    u = _dot(xn, w_ref[:, 2 * cw:3 * cw])
    zvf = _dot(xn, w_ref[:, o_v:o_v + ATTN_WIDTH + LANES])
    gate_b = _dot(xn, w_ref[:, 0:cw])
    q_ref[...] = (_dot(xn, w_ref[:, o_q:o_q + ATTN_WIDTH]) * (GROUP_DIM ** -0.5)).astype(BF16)
    v_ref[...] = zvf[:, 0:ATTN_WIDTH].astype(BF16)

    gcu = gate_c * u
    carry = conv_carry[...]
    prev1 = pltpu.roll(gcu, 1, axis=0)
    prev2 = pltpu.roll(gcu, 2, axis=0)
    row8 = lax.broadcasted_iota(jnp.int32, (SUBLANES, cw), 0)
    head1 = jnp.where(row8 < 1, pltpu.roll(carry, 1, axis=0), prev1[0:SUBLANES])
    head2 = jnp.where(row8 < 2, pltpu.roll(carry, 2, axis=0), prev2[0:SUBLANES])
    prev1 = jnp.concatenate([head1, prev1[SUBLANES:]], axis=0)
    prev2 = jnp.concatenate([head2, prev2[SUBLANES:]], axis=0)
    conv_carry[...] = gcu[tm - SUBLANES:tm]
    conv = cw_ref[0:1, :] * prev2 + cw_ref[1:2, :] * prev1 + cw_ref[2:3, :] * gcu
    yc = gate_b * conv
    y2 = yc * yc
    lane = lax.broadcasted_iota(jnp.int32, (tm, LANES), 1)
    low = lane < GROUP_DIM
    ms = []
    for c0 in range(0, cw, LANES):
        blk = y2[:, c0:c0 + LANES]
        s_lo = jnp.sum(jnp.where(low, blk, 0.0), axis=-1, keepdims=True)
        s_hi = jnp.sum(jnp.where(low, 0.0, blk), axis=-1, keepdims=True)
        ms.append(jnp.where(low, s_lo, s_hi) * (1.0 / GROUP_DIM))
    ms = jnp.concatenate(ms, axis=1)
    yc_ref[...] = (yc * lax.rsqrt(ms + EPS) * gmix_ref[...]).astype(BF16)
    k_ref[...] = _dot(xn, w_ref[:, o_q + ATTN_WIDTH:o_q + 2 * ATTN_WIDTH]).astype(BF16)

    zf = zvf[:, ATTN_WIDTH:] + bf_ref[...]
    lf = -(jnp.maximum(-zf, 0.0) + jnp.log1p(jnp.exp(-jnp.abs(zf))))
    lft = lf.T[0:N_HEADS, :]
    hi = lft.astype(BF16).astype(F32)
    r1 = lft - hi
    mid = r1.astype(BF16).astype(F32)
    lo = r1 - mid
    parts = jnp.concatenate([hi, mid, lo], axis=0).astype(BF16)
    r_i = lax.broadcasted_iota(jnp.int32, (tm, tm), 0)
    c_i = lax.broadcasted_iota(jnp.int32, (tm, tm), 1)
    triu = jnp.where(r_i <= c_i, 1.0, 0.0).astype(BF16)
    cs = _dot(parts, triu)
    c = c_carry[...] + ((cs[0:N_HEADS] + cs[N_HEADS:2 * N_HEADS]) + cs[2 * N_HEADS:])
    c_carry[...] = jnp.broadcast_to(c[:, tm - 1:tm], c_carry.shape)
    crow_ref[...] = c


def _in_proj(x, g, w, bf, cw, gmix):
    B, S, _ = x.shape
    tm = TM_IN
    n_cols = w.shape[1]
    const = lambda b, i: (0, 0)
    single = dict(pipeline_mode=pl.Buffered(1))
    tok = lambda width: pl.BlockSpec((None, tm, width), lambda b, i: (b, i, 0))
    return pl.pallas_call(
        _in_proj_kernel,
        grid=(B, S // tm),
        in_specs=[
            tok(D_MODEL),
            pl.BlockSpec((1, D_MODEL), const, **single),
            pl.BlockSpec((D_MODEL, n_cols), const, **single),
            pl.BlockSpec((1, LANES), const, **single),
            pl.BlockSpec((CONV_K, CONV_WIDTH), const, **single),
            pl.BlockSpec((1, CONV_WIDTH), const, **single),
        ],
        out_specs=[
            tok(CONV_WIDTH), tok(ATTN_WIDTH), tok(ATTN_WIDTH), tok(ATTN_WIDTH),
            pl.BlockSpec((None, N_HEADS, tm), lambda b, i: (b, 0, i)),
        ],
        out_shape=[
            jax.ShapeDtypeStruct((B, S, CONV_WIDTH), BF16),
            jax.ShapeDtypeStruct((B, S, ATTN_WIDTH), BF16),
            jax.ShapeDtypeStruct((B, S, ATTN_WIDTH), BF16),
            jax.ShapeDtypeStruct((B, S, ATTN_WIDTH), BF16),
            jax.ShapeDtypeStruct((B, N_HEADS, S), F32),
        ],
        scratch_shapes=[
            pltpu.VMEM((SUBLANES, CONV_WIDTH), F32),
            pltpu.VMEM((N_HEADS, TM_IN), F32),
        ],
        compiler_params=pltpu.CompilerParams(
            dimension_semantics=("arbitrary", "arbitrary"),
            vmem_limit_bytes=VMEM_LIMIT_BYTES),
        name="in_proj",
    )(x, g, w, bf, cw, gmix)


def _attn_kernel(q_ref, k_ref, v_ref, crow_ref, gmix_ref, o_ref,
                 s_scr, acc_scr, mrun_scr, m_scr, q2_scr):
    assert TQ == TK, "one key tile per query block sits on the diagonal"
    S = q_ref.shape[0]
    nq = S // TQ
    half = TQ // 2
    pairs = range(PAIRS_PER_STEP)
    heads = range(HEADS_PER_BLOCK)
    head0 = pl.program_id(1) * (PAIRS_PER_STEP * HEADS_PER_BLOCK)
    lane = lax.broadcasted_iota(jnp.int32, (TQ, LANES), 1)
    head_lanes = [(lane >= hh * GROUP_DIM) & (lane < (hh + 1) * GROUP_DIM) for hh in heads]
    ones = jnp.ones((TK, LANES), BF16)
    top_rows = [slice(hh * TQ, hh * TQ + half) for hh in heads]
    bot_rows = [slice(hh * TQ + half, (hh + 1) * TQ) for hh in heads]

    acc_scr[...] = jnp.zeros_like(acc_scr)
    mrun_scr[...] = jnp.full(mrun_scr.shape, MASK_VALUE, F32)

    def crow(pp, hh, start, size):
        return crow_ref[pl.ds(head0 + pp * HEADS_PER_BLOCK + hh, 1), pl.ds(start, size)]

    def lane_block_max(t):
        tmax = t[:, 0:LANES]
        for c0 in range(LANES, t.shape[1], LANES):
            tmax = jnp.maximum(tmax, t[:, c0:c0 + LANES])
        return tmax

    def exp_pv(pp, row_slices, s0, n_keys):
        m = jnp.concatenate([m_scr[pp, r, :] for r in row_slices], axis=0)
        s = jnp.concatenate([s_scr[pp, r, pl.ds(s0, n_keys)] for r in row_slices], axis=0)
        p = jnp.concatenate(
            [jnp.exp(s[:, c0:c0 + LANES] - m) for c0 in range(0, n_keys, LANES)], axis=1)
        v_aug = jnp.concatenate(
            [v_ref[pl.ds(s0, n_keys), pp * LANES:(pp + 1) * LANES], ones[0:n_keys]], axis=1)
        pv = _dot(p.astype(BF16), v_aug)
        r0 = 0
        for r in row_slices:
            n = r.stop - r.start
            acc_scr[pp, r, :] += pv[r0:r0 + n]
            r0 += n

    def pass2_tile(pp, s0):
        exp_pv(pp, [slice(0, HEADS_PER_BLOCK * TQ)], s0, TK)

    def pass2_diag(pp, s0):
        exp_pv(pp, top_rows, s0, half)
        exp_pv(pp, bot_rows, s0, TK)

    def pass1_tile(pp, s0, crefs):
        raw = lax.dot_general(q2_scr[pp], k_ref[pl.ds(s0, TK), pp * LANES:(pp + 1) * LANES],
                              (((1,), (1,)), ((), ())), preferred_element_type=F32)
        parts = [raw[hh * TQ:(hh + 1) * TQ] + (crefs[pp][hh] - crow(pp, hh, s0, TK))
                 for hh in heads]
        t = jnp.concatenate(parts, axis=0)
        s_scr[pp, :, pl.ds(s0, TK)] = t
        mrun_scr[pp] = jnp.maximum(mrun_scr[pp], lane_block_max(t))

    def pass1_diag(pp, t0, crefs):
        nt = (((1,), (1,)), ((), ()))
        t1 = pl.multiple_of(t0 + half, half)
        k_lo = k_ref[pl.ds(t0, half), pp * LANES:(pp + 1) * LANES]
        k_hi = k_ref[pl.ds(t1, half), pp * LANES:(pp + 1) * LANES]
        raw_lo = lax.dot_general(q2_scr[pp], k_lo, nt, preferred_element_type=F32)
        q_bot = jnp.concatenate([q2_scr[pp, r, :] for r in bot_rows], axis=0)
        raw_hi = lax.dot_general(q_bot, k_hi, nt, preferred_element_type=F32)
        causal = (lax.broadcasted_iota(jnp.int32, (half, half), 1)
                  <= lax.broadcasted_iota(jnp.int32, (half, half), 0))
        for hh in heads:
            bias_lo = crefs[pp][hh] - crow(pp, hh, t0, half)
            bias_hi = crefs[pp][hh] - crow(pp, hh, t1, half)
            top = jnp.where(causal, raw_lo[top_rows[hh]] + bias_lo, MASK_VALUE)
            bot_lo = raw_lo[bot_rows[hh]] + bias_lo
            bot_hi = jnp.where(causal, raw_hi[hh * half:(hh + 1) * half] + bias_hi, MASK_VALUE)
            s_scr[pp, top_rows[hh], pl.ds(t0, half)] = top
            s_scr[pp, bot_rows[hh], pl.ds(t0, half)] = bot_lo
            s_scr[pp, bot_rows[hh], pl.ds(t1, half)] = bot_hi
            mrun_scr[pp, top_rows[hh], :] = jnp.maximum(mrun_scr[pp, top_rows[hh], :],
                                                        lane_block_max(top))
            mrun_scr[pp, bot_rows[hh], :] = jnp.maximum(
                mrun_scr[pp, bot_rows[hh], :],
                jnp.maximum(lane_block_max(bot_lo), lane_block_max(bot_hi)))

    def finalize(t0):
        for pp in pairs:
            out = None
            for hh in heads:
                a = acc_scr[pp, hh * TQ:(hh + 1) * TQ, :]
                o = jnp.where(head_lanes[hh], a[:, 0:LANES] / a[:, LANES:], 0.0)
                ms = jnp.sum(o * o, axis=-1, keepdims=True) * (1.0 / GROUP_DIM)
                o = o * lax.rsqrt(ms + EPS)
                out = o if out is None else out + o
            o_ref[pl.ds(t0, TQ), pp * LANES:(pp + 1) * LANES] = (
                out * gmix_ref[:, pp * LANES:(pp + 1) * LANES]).astype(BF16)
        acc_scr[...] = jnp.zeros_like(acc_scr)

    def q_block(qi, first):
        t0 = pl.multiple_of(qi * TQ, TQ)
        crefs = []
        for pp in pairs:
            q = q_ref[pl.ds(t0, TQ), pp * LANES:(pp + 1) * LANES]
            crefs.append([])
            for hh in heads:
                q2_scr[pp, hh * TQ:(hh + 1) * TQ, :] = jnp.where(head_lanes[hh], q,
                                                                 jnp.zeros_like(q))
                crefs[pp].append(jnp.min(crow(pp, hh, t0, TQ), axis=-1, keepdims=True))

        def both(j, n_tiles):
            starts = [pl.multiple_of((j * n_tiles + d) * TK, TK) for d in range(n_tiles)]
            for s0 in starts:
                for pp in pairs:
                    pass2_tile(pp, s0)
            for s0 in starts:
                for pp in pairs:
                    pass1_tile(pp, s0, crefs)
            return 0

        if not first:
            n_full = qi - 1
            n_steps = n_full // TILE_UNROLL
            lax.fori_loop(0, n_steps, lambda j, _: both(j, TILE_UNROLL), 0)
            lax.fori_loop(n_steps * TILE_UNROLL, n_full, lambda j, _: both(j, 1), 0)
            s_prev = pl.multiple_of(t0 - TQ, TQ)
            for pp in pairs:
                pass2_diag(pp, s_prev)
                pass1_tile(pp, s_prev, crefs)
            finalize(s_prev)

        for pp in pairs:
            pass1_diag(pp, t0, crefs)
        for pp in pairs:
            m = jnp.max(mrun_scr[pp], axis=-1, keepdims=True)
            m_scr[pp] = jnp.broadcast_to(m, m_scr.shape[1:])
        mrun_scr[...] = jnp.full(mrun_scr.shape, MASK_VALUE, F32)
        return 0

    q_block(0, True)
    lax.fori_loop(1, nq, lambda qi, _: q_block(qi, False), 0)

    def drain(j, _):
        for d in range(DRAIN_UNROLL):
            for pp in pairs:
                pass2_tile(pp, pl.multiple_of((j * DRAIN_UNROLL + d) * TK, TK))
        return 0

    n_steps = (nq - 1) // DRAIN_UNROLL
    lax.fori_loop(0, n_steps, drain, 0)
    for j in range(n_steps * DRAIN_UNROLL, nq - 1):
        for pp in pairs:
            pass2_tile(pp, j * TK)
    for pp in pairs:
        pass2_diag(pp, S - TQ)
    finalize(S - TQ)


def _attention(q, k, v, crow, gmix_attn):
    B, S, _ = q.shape
    n_pairs = N_HEADS // HEADS_PER_BLOCK
    rows = HEADS_PER_BLOCK * TQ
    width = PAIRS_PER_STEP * LANES
    seq = pl.BlockSpec((None, S, width), lambda b, j: (b, 0, j))
    return pl.pallas_call(
        _attn_kernel,
        grid=(B, n_pairs // PAIRS_PER_STEP),
        in_specs=[
            seq, seq, seq,
            pl.BlockSpec((None, N_HEADS, S), lambda b, j: (b, 0, 0)),
            pl.BlockSpec((1, width), lambda b, j: (0, j)),
        ],
        out_specs=seq,
        out_shape=jax.ShapeDtypeStruct((B, S, ATTN_WIDTH), BF16),
        scratch_shapes=[
            pltpu.VMEM((PAIRS_PER_STEP, rows, S), F32),
            pltpu.VMEM((PAIRS_PER_STEP, rows, 2 * LANES), F32),
            pltpu.VMEM((PAIRS_PER_STEP, rows, LANES), F32),
            pltpu.VMEM((PAIRS_PER_STEP, rows, LANES), F32),
            pltpu.VMEM((PAIRS_PER_STEP, rows, LANES), BF16),
        ],
        compiler_params=pltpu.CompilerParams(
            dimension_semantics=("arbitrary", "arbitrary"),
            vmem_limit_bytes=VMEM_LIMIT_BYTES),
        name="forgetting_attention",
    )(q, k, v, crow, gmix_attn)


def _out_kernel(x_ref, yc_ref, ya_ref, p_ref, wo_ref, gffn_ref, wgu_ref, wd_ref,
                gple_ref, wpg_ref, bpg_ref, wpp_ref, gfin_ref, o_ref):
    y = jnp.concatenate([yc_ref[...], ya_ref[...]], axis=-1)
    h = x_ref[...] + _dot(y, wo_ref[...])
    hn = (h * _rms_scale(h) * gffn_ref[...]).astype(BF16)
    ff = None
    for c0 in range(0, D_FF, FF_CHUNK):
        g = _dot(hn, wgu_ref[:, c0:c0 + FF_CHUNK])
        up = _dot(hn, wgu_ref[:, D_FF + c0:D_FF + c0 + FF_CHUNK])
        a = (g * jax.nn.sigmoid(g) * up).astype(BF16)
        d = _dot(a, wd_ref[c0:c0 + FF_CHUNK, :])
        ff = d if ff is None else ff + d
    h = h + ff
    hn = (h * _rms_scale(h) * gple_ref[...]).astype(BF16)
    gate = jax.nn.sigmoid(_dot(hn, wpg_ref[...]) + bpg_ref[...])
    h = h + gate * _dot(p_ref[...].astype(BF16), wpp_ref[...])
    o_ref[...] = h * _rms_scale(h) * gfin_ref[...]


def _out_block(x, yc, ya, p, wo, gffn, wgu, wd, gple, wpg, bpg, wpp, gfin):
    n_tok = x.shape[0]
    tm = TM_OUT
    const = lambda i: (0, 0)
    single = dict(pipeline_mode=pl.Buffered(1))
    tok = lambda width: pl.BlockSpec((tm, width), lambda i: (i, 0))
    full = lambda a: pl.BlockSpec(a.shape, const, **single)
    return pl.pallas_call(
        _out_kernel,
        grid=(n_tok // tm,),
        in_specs=[tok(D_MODEL), tok(CONV_WIDTH), tok(ATTN_WIDTH), tok(PLE_DIM),
                  full(wo), full(gffn), full(wgu), full(wd), full(gple), full(wpg),
                  full(bpg), full(wpp), full(gfin)],
        out_specs=tok(D_MODEL),
        out_shape=jax.ShapeDtypeStruct((n_tok, D_MODEL), F32),
        compiler_params=pltpu.CompilerParams(
            dimension_semantics=("arbitrary",),
            vmem_limit_bytes=VMEM_LIMIT_BYTES),
        name="out_ffn_ple",
    )(x, yc, ya, p, wo, gffn, wgu, wd, gple, wpg, bpg, wpp, gfin)


def _layer(h, p_i, mix_norm, w_in, b_f, conv_w, mix_out_norm, w_o, ffn_norm, w_gate_up,
           w_down, ple_norm, w_ple_gate, b_ple_gate, w_ple_proj, out_norm):
    B, S, _ = h.shape
    row = lambda a: a.reshape(1, -1).astype(F32)
    n_main = 3 * CONV_WIDTH + 3 * ATTN_WIDTH
    w = jnp.pad(w_in, ((0, 0), (0, LANES - N_HEADS))).astype(BF16)
    bf = jnp.pad(row(b_f), ((0, 0), (0, LANES - N_HEADS)))
    assert w.shape[1] == n_main + LANES
    gmix = row(mix_out_norm)
    yc, q, k, v, crow = _in_proj(h, row(mix_norm), w, bf, conv_w.astype(F32),
                                 gmix[:, :CONV_WIDTH])
    ya = _attention(q, k, v, crow, gmix[:, CONV_WIDTH:])
    n_tok = B * S
    out = _out_block(
        h.reshape(n_tok, D_MODEL), yc.reshape(n_tok, CONV_WIDTH),
        ya.reshape(n_tok, ATTN_WIDTH), p_i.reshape(n_tok, PLE_DIM),
        w_o.astype(BF16), row(ffn_norm), w_gate_up.astype(BF16), w_down.astype(BF16),
        row(ple_norm), w_ple_gate.astype(BF16), row(b_ple_gate), w_ple_proj.astype(BF16),
        row(out_norm))
    return out.reshape(B, S, D_MODEL)


def kernel(x, p, mix_norm, w_in, b_f, conv_w, mix_out_norm, w_o, ffn_norm, w_gate_up,
           w_down, ple_norm, w_ple_gate, b_ple_gate, w_ple_proj, final_norm):
    depth = p.shape[0]
    assert depth == 1, "the final RMSNorm is fused into the single layer's last kernel"
    return _layer(x, p[0], mix_norm[0], w_in[0], b_f[0], conv_w[0], mix_out_norm[0],
                  w_o[0], ffn_norm[0], w_gate_up[0], w_down[0], ple_norm[0],
                  w_ple_gate[0], b_ple_gate[0], w_ple_proj[0], final_norm)
```

```python
import jax
import jax.numpy as jnp
from jax import lax
from jax.experimental import pallas as pl
from jax.experimental.pallas import tpu as pltpu

D_MODEL = 1024
PLE_DIM = 256
CONV_WIDTH = 512
ATTN_WIDTH = 512
GROUP_DIM = 64
N_HEADS = ATTN_WIDTH // GROUP_DIM
CONV_K = 3
D_FF = 2816
EPS = 1e-6

LANES = 128
SUBLANES = 8
HEADS_PER_BLOCK = LANES // GROUP_DIM
VMEM_LIMIT_BYTES = 56 * 1024 * 1024

TM_IN = 512
TQ = 512
TK = 512
PAIRS_PER_STEP = 2
TILE_UNROLL = 2
DRAIN_UNROLL = 2
TM_OUT = 512
FF_CHUNK = 256
MASK_VALUE = -1e30

F32 = jnp.float32
BF16 = jnp.bfloat16


def _rms_scale(x):
    return lax.rsqrt(jnp.mean(x * x, axis=-1, keepdims=True) + EPS)


def _dot(a, b):
    return jnp.dot(a, b, preferred_element_type=F32)


def _in_proj_kernel(x_ref, g_ref, w_ref, bf_ref, cw_ref, gmix_ref,
                    yc_ref, q_ref, k_ref, v_ref, crow_ref,
                    conv_carry, c_carry):
    tm = x_ref.shape[0]

    @pl.when(pl.program_id(1) == 0)
    def _():
        conv_carry[...] = jnp.zeros_like(conv_carry)
        c_carry[...] = jnp.zeros_like(c_carry)

    x = x_ref[...]
    xn = (x * _rms_scale(x) * g_ref[...]).astype(BF16)

    cw = CONV_WIDTH
    o_q = 3 * cw
    o_v = o_q + 2 * ATTN_WIDTH
    gate_c = _dot(xn, w_ref[:, cw:2 * cw])
    u = _dot(xn, w_ref[:, 2 * cw:3 * cw])
    zvf = _dot(xn, w_ref[:, o_v:o_v + ATTN_WIDTH + LANES])
    gate_b = _dot(xn, w_ref[:, 0:cw])
    q_ref[...] = (_dot(xn, w_ref[:, o_q:o_q + ATTN_WIDTH]) * (GROUP_DIM ** -0.5)).astype(BF16)
    v_ref[...] = zvf[:, 0:ATTN_WIDTH].astype(BF16)

    gcu = gate_c * u
    carry = conv_carry[...]
    prev1 = pltpu.roll(gcu, 1, axis=0)
    prev2 = pltpu.roll(gcu, 2, axis=0)
    row8 = lax.broadcasted_iota(jnp.int32, (SUBLANES, cw), 0)
    head1 = jnp.where(row8 < 1, pltpu.roll(carry, 1, axis=0), prev1[0:SUBLANES])
    head2 = jnp.where(row8 < 2, pltpu.roll(carry, 2, axis=0), prev2[0:SUBLANES])
    prev1 = jnp.concatenate([head1, prev1[SUBLANES:]], axis=0)
    prev2 = jnp.concatenate([head2, prev2[SUBLANES:]], axis=0)
    conv_carry[...] = gcu[tm - SUBLANES:tm]
    conv = cw_ref[0:1, :] * prev2 + cw_ref[1:2, :] * prev1 + cw_ref[2:3, :] * gcu
    yc = gate_b * conv
    y2 = yc * yc
    lane = lax.broadcasted_iota(jnp.int32, (tm, LANES), 1)
    low = lane < GROUP_DIM
    ms = []
    for c0 in range(0, cw, LANES):
        blk = y2[:, c0:c0 + LANES]
        s_lo = jnp.sum(jnp.where(low, blk, 0.0), axis=-1, keepdims=True)
        s_hi = jnp.sum(jnp.where(low, 0.0, blk), axis=-1, keepdims=True)
        ms.append(jnp.where(low, s_lo, s_hi) * (1.0 / GROUP_DIM))
    ms = jnp.concatenate(ms, axis=1)
    yc_ref[...] = (yc * lax.rsqrt(ms + EPS) * gmix_ref[...]).astype(BF16)
    k_ref[...] = _dot(xn, w_ref[:, o_q + ATTN_WIDTH:o_q + 2 * ATTN_WIDTH]).astype(BF16)

    zf = zvf[:, ATTN_WIDTH:] + bf_ref[...]
    lf = -(jnp.maximum(-zf, 0.0) + jnp.log1p(jnp.exp(-jnp.abs(zf))))
    lft = lf.T[0:N_HEADS, :]
    hi = lft.astype(BF16).astype(F32)
    r1 = lft - hi
    mid = r1.astype(BF16).astype(F32)
    lo = r1 - mid
    parts = jnp.concatenate([hi, mid, lo], axis=0).astype(BF16)
    r_i = lax.broadcasted_iota(jnp.int32, (tm, tm), 0)
    c_i = lax.broadcasted_iota(jnp.int32, (tm, tm), 1)
    triu = jnp.where(r_i <= c_i, 1.0, 0.0).astype(BF16)
    cs = _dot(parts, triu)
    c = c_carry[...] + ((cs[0:N_HEADS] + cs[N_HEADS:2 * N_HEADS]) + cs[2 * N_HEADS:])
    c_carry[...] = jnp.broadcast_to(c[:, tm - 1:tm], c_carry.shape)
    crow_ref[...] = c


def _in_proj(x, g, w, bf, cw, gmix):
    B, S, _ = x.shape
    tm = TM_IN
    n_cols = w.shape[1]
    const = lambda b, i: (0, 0)
    single = dict(pipeline_mode=pl.Buffered(1))
    tok = lambda width: pl.BlockSpec((None, tm, width), lambda b, i: (b, i, 0))
    return pl.pallas_call(
        _in_proj_kernel,
        grid=(B, S // tm),
        in_specs=[
            tok(D_MODEL),
            pl.BlockSpec((1, D_MODEL), const, **single),
            pl.BlockSpec((D_MODEL, n_cols), const, **single),
            pl.BlockSpec((1, LANES), const, **single),
            pl.BlockSpec((CONV_K, CONV_WIDTH), const, **single),
            pl.BlockSpec((1, CONV_WIDTH), const, **single),
        ],
        out_specs=[
            tok(CONV_WIDTH), tok(ATTN_WIDTH), tok(ATTN_WIDTH), tok(ATTN_WIDTH),
            pl.BlockSpec((None, N_HEADS, tm), lambda b, i: (b, 0, i)),
        ],
        out_shape=[
            jax.ShapeDtypeStruct((B, S, CONV_WIDTH), BF16),
            jax.ShapeDtypeStruct((B, S, ATTN_WIDTH), BF16),
            jax.ShapeDtypeStruct((B, S, ATTN_WIDTH), BF16),
            jax.ShapeDtypeStruct((B, S, ATTN_WIDTH), BF16),
            jax.ShapeDtypeStruct((B, N_HEADS, S), F32),
        ],
        scratch_shapes=[
            pltpu.VMEM((SUBLANES, CONV_WIDTH), F32),
            pltpu.VMEM((N_HEADS, TM_IN), F32),
        ],
        compiler_params=pltpu.CompilerParams(
            dimension_semantics=("arbitrary", "arbitrary"),
            vmem_limit_bytes=VMEM_LIMIT_BYTES),
        name="in_proj",
    )(x, g, w, bf, cw, gmix)


def _attn_kernel(q_ref, k_ref, v_ref, crow_ref, gmix_ref, o_ref,
                 s_scr, acc_scr, mrun_scr, m_scr, q2_scr):
    assert TQ == TK, "one key tile per query block sits on the diagonal"
    S = q_ref.shape[0]
    nq = S // TQ
    half = TQ // 2
    pairs = range(PAIRS_PER_STEP)
    heads = range(HEADS_PER_BLOCK)
    head0 = pl.program_id(1) * (PAIRS_PER_STEP * HEADS_PER_BLOCK)
    lane = lax.broadcasted_iota(jnp.int32, (TQ, LANES), 1)
    head_lanes = [(lane >= hh * GROUP_DIM) & (lane < (hh + 1) * GROUP_DIM) for hh in heads]
    ones = jnp.ones((TK, LANES), BF16)
    top_rows = [slice(hh * TQ, hh * TQ + half) for hh in heads]
    bot_rows = [slice(hh * TQ + half, (hh + 1) * TQ) for hh in heads]

    acc_scr[...] = jnp.zeros_like(acc_scr)
    mrun_scr[...] = jnp.full(mrun_scr.shape, MASK_VALUE, F32)

    def crow(pp, hh, start, size):
        return crow_ref[pl.ds(head0 + pp * HEADS_PER_BLOCK + hh, 1), pl.ds(start, size)]

    def lane_block_max(t):
        tmax = t[:, 0:LANES]
        for c0 in range(LANES, t.shape[1], LANES):
            tmax = jnp.maximum(tmax, t[:, c0:c0 + LANES])
        return tmax

    def exp_pv(pp, row_slices, s0, n_keys):
        m = jnp.concatenate([m_scr[pp, r, :] for r in row_slices], axis=0)
        s = jnp.concatenate([s_scr[pp, r, pl.ds(s0, n_keys)] for r in row_slices], axis=0)
        p = jnp.concatenate(
            [jnp.exp(s[:, c0:c0 + LANES] - m) for c0 in range(0, n_keys, LANES)], axis=1)
        v_aug = jnp.concatenate(
            [v_ref[pl.ds(s0, n_keys), pp * LANES:(pp + 1) * LANES], ones[0:n_keys]], axis=1)
        pv = _dot(p.astype(BF16), v_aug)
        r0 = 0
        for r in row_slices:
            n = r.stop - r.start
            acc_scr[pp, r, :] += pv[r0:r0 + n]
            r0 += n

    def pass2_tile(pp, s0):
        exp_pv(pp, [slice(0, HEADS_PER_BLOCK * TQ)], s0, TK)

    def pass2_diag(pp, s0):
        exp_pv(pp, top_rows, s0, half)
        exp_pv(pp, bot_rows, s0, TK)

    def pass1_tile(pp, s0, crefs):
        raw = lax.dot_general(q2_scr[pp], k_ref[pl.ds(s0, TK), pp * LANES:(pp + 1) * LANES],
                              (((1,), (1,)), ((), ())), preferred_element_type=F32)
        parts = [raw[hh * TQ:(hh + 1) * TQ] + (crefs[pp][hh] - crow(pp, hh, s0, TK))
                 for hh in heads]
        t = jnp.concatenate(parts, axis=0)
        s_scr[pp, :, pl.ds(s0, TK)] = t
        mrun_scr[pp] = jnp.maximum(mrun_scr[pp], lane_block_max(t))

    def pass1_diag(pp, t0, crefs):
        nt = (((1,), (1,)), ((), ()))
        t1 = pl.multiple_of(t0 + half, half)
        k_lo = k_ref[pl.ds(t0, half), pp * LANES:(pp + 1) * LANES]
        k_hi = k_ref[pl.ds(t1, half), pp * LANES:(pp + 1) * LANES]
        raw_lo = lax.dot_general(q2_scr[pp], k_lo, nt, preferred_element_type=F32)
        q_bot = jnp.concatenate([q2_scr[pp, r, :] for r in bot_rows], axis=0)
        raw_hi = lax.dot_general(q_bot, k_hi, nt, preferred_element_type=F32)
        causal = (lax.broadcasted_iota(jnp.int32, (half, half), 1)
                  <= lax.broadcasted_iota(jnp.int32, (half, half), 0))
        for hh in heads:
            bias_lo = crefs[pp][hh] - crow(pp, hh, t0, half)
            bias_hi = crefs[pp][hh] - crow(pp, hh, t1, half)
            top = jnp.where(causal, raw_lo[top_rows[hh]] + bias_lo, MASK_VALUE)
            bot_lo = raw_lo[bot_rows[hh]] + bias_lo
            bot_hi = jnp.where(causal, raw_hi[hh * half:(hh + 1) * half] + bias_hi, MASK_VALUE)
            s_scr[pp, top_rows[hh], pl.ds(t0, half)] = top
            s_scr[pp, bot_rows[hh], pl.ds(t0, half)] = bot_lo
            s_scr[pp, bot_rows[hh], pl.ds(t1, half)] = bot_hi
            mrun_scr[pp, top_rows[hh], :] = jnp.maximum(mrun_scr[pp, top_rows[hh], :],
                                                        lane_block_max(top))
            mrun_scr[pp, bot_rows[hh], :] = jnp.maximum(
                mrun_scr[pp, bot_rows[hh], :],
                jnp.maximum(lane_block_max(bot_lo), lane_block_max(bot_hi)))

    def finalize(t0):
        for pp in pairs:
            out = None
            for hh in heads:
                a = acc_scr[pp, hh * TQ:(hh + 1) * TQ, :]
                o = jnp.where(head_lanes[hh], a[:, 0:LANES] / a[:, LANES:], 0.0)
                ms = jnp.sum(o * o, axis=-1, keepdims=True) * (1.0 / GROUP_DIM)
                o = o * lax.rsqrt(ms + EPS)
                out = o if out is None else out + o
            o_ref[pl.ds(t0, TQ), pp * LANES:(pp + 1) * LANES] = (
                out * gmix_ref[:, pp * LANES:(pp + 1) * LANES]).astype(BF16)
        acc_scr[...] = jnp.zeros_like(acc_scr)

    def q_block(qi, first):
        t0 = pl.multiple_of(qi * TQ, TQ)
        crefs = []
        for pp in pairs:
            q = q_ref[pl.ds(t0, TQ), pp * LANES:(pp + 1) * LANES]
            crefs.append([])
            for hh in heads:
                q2_scr[pp, hh * TQ:(hh + 1) * TQ, :] = jnp.where(head_lanes[hh], q,
                                                                 jnp.zeros_like(q))
                crefs[pp].append(jnp.min(crow(pp, hh, t0, TQ), axis=-1, keepdims=True))

        def both(j, n_tiles):
            starts = [pl.multiple_of((j * n_tiles + d) * TK, TK) for d in range(n_tiles)]
            for s0 in starts:
                for pp in pairs:
                    pass2_tile(pp, s0)
            for s0 in starts:
                for pp in pairs:
                    pass1_tile(pp, s0, crefs)
            return 0

        if not first:
            n_full = qi - 1
            n_steps = n_full // TILE_UNROLL
            lax.fori_loop(0, n_steps, lambda j, _: both(j, TILE_UNROLL), 0)
            lax.fori_loop(n_steps * TILE_UNROLL, n_full, lambda j, _: both(j, 1), 0)
            s_prev = pl.multiple_of(t0 - TQ, TQ)
            for pp in pairs:
                pass2_diag(pp, s_prev)
                pass1_tile(pp, s_prev, crefs)
                pass1_diag(pp, t0, crefs)
            finalize(s_prev)
        else:
            for pp in pairs:
                pass1_diag(pp, t0, crefs)
        for pp in pairs:
            m = jnp.max(mrun_scr[pp], axis=-1, keepdims=True)
            m_scr[pp] = jnp.broadcast_to(m, m_scr.shape[1:])
        mrun_scr[...] = jnp.full(mrun_scr.shape, MASK_VALUE, F32)
        return 0

    q_block(0, True)
    lax.fori_loop(1, nq, lambda qi, _: q_block(qi, False), 0)

    def drain(j, _):
        for d in range(DRAIN_UNROLL):
            for pp in pairs:
                pass2_tile(pp, pl.multiple_of((j * DRAIN_UNROLL + d) * TK, TK))
        return 0

    n_steps = (nq - 1) // DRAIN_UNROLL
    lax.fori_loop(0, n_steps, drain, 0)
    for j in range(n_steps * DRAIN_UNROLL, nq - 1):
        for pp in pairs:
            pass2_tile(pp, j * TK)
    for pp in pairs:
        pass2_diag(pp, S - TQ)
    finalize(S - TQ)


def _attention(q, k, v, crow, gmix_attn):
    B, S, _ = q.shape
    n_pairs = N_HEADS // HEADS_PER_BLOCK
    rows = HEADS_PER_BLOCK * TQ
    width = PAIRS_PER_STEP * LANES
    seq = pl.BlockSpec((None, S, width), lambda b, j: (b, 0, j))
    return pl.pallas_call(
        _attn_kernel,
        grid=(B, n_pairs // PAIRS_PER_STEP),
        in_specs=[
            seq, seq, seq,
            pl.BlockSpec((None, N_HEADS, S), lambda b, j: (b, 0, 0)),
            pl.BlockSpec((1, width), lambda b, j: (0, j)),
        ],
        out_specs=seq,
        out_shape=jax.ShapeDtypeStruct((B, S, ATTN_WIDTH), BF16),
        scratch_shapes=[
            pltpu.VMEM((PAIRS_PER_STEP, rows, S), F32),
            pltpu.VMEM((PAIRS_PER_STEP, rows, 2 * LANES), F32),
            pltpu.VMEM((PAIRS_PER_STEP, rows, LANES), F32),
            pltpu.VMEM((PAIRS_PER_STEP, rows, LANES), F32),
            pltpu.VMEM((PAIRS_PER_STEP, rows, LANES), BF16),
        ],
        compiler_params=pltpu.CompilerParams(
            dimension_semantics=("arbitrary", "arbitrary"),
            vmem_limit_bytes=VMEM_LIMIT_BYTES),
        name="forgetting_attention",
    )(q, k, v, crow, gmix_attn)


def _out_kernel(x_ref, yc_ref, ya_ref, p_ref, wo_ref, gffn_ref, wgu_ref, wd_ref,
                gple_ref, wpg_ref, bpg_ref, wpp_ref, gfin_ref, o_ref):
    y = jnp.concatenate([yc_ref[...], ya_ref[...]], axis=-1)
    h = x_ref[...] + _dot(y, wo_ref[...])
    hn = (h * _rms_scale(h) * gffn_ref[...]).astype(BF16)
    ff = None
    for c0 in range(0, D_FF, FF_CHUNK):
        g = _dot(hn, wgu_ref[:, c0:c0 + FF_CHUNK])
        up = _dot(hn, wgu_ref[:, D_FF + c0:D_FF + c0 + FF_CHUNK])
        a = (g * jax.nn.sigmoid(g) * up).astype(BF16)
        d = _dot(a, wd_ref[c0:c0 + FF_CHUNK, :])
        ff = d if ff is None else ff + d
    h = h + ff
    hn = (h * _rms_scale(h) * gple_ref[...]).astype(BF16)
    gate = jax.nn.sigmoid(_dot(hn, wpg_ref[...]) + bpg_ref[...])
    h = h + gate * _dot(p_ref[...].astype(BF16), wpp_ref[...])
    o_ref[...] = h * _rms_scale(h) * gfin_ref[...]


def _out_block(x, yc, ya, p, wo, gffn, wgu, wd, gple, wpg, bpg, wpp, gfin):
    n_tok = x.shape[0]
    tm = TM_OUT
    const = lambda i: (0, 0)
    single = dict(pipeline_mode=pl.Buffered(1))
    tok = lambda width: pl.BlockSpec((tm, width), lambda i: (i, 0))
    full = lambda a: pl.BlockSpec(a.shape, const, **single)
    return pl.pallas_call(
        _out_kernel,
        grid=(n_tok // tm,),
        in_specs=[tok(D_MODEL), tok(CONV_WIDTH), tok(ATTN_WIDTH), tok(PLE_DIM),
                  full(wo), full(gffn), full(wgu), full(wd), full(gple), full(wpg),
                  full(bpg), full(wpp), full(gfin)],
        out_specs=tok(D_MODEL),
        out_shape=jax.ShapeDtypeStruct((n_tok, D_MODEL), F32),
        compiler_params=pltpu.CompilerParams(
            dimension_semantics=("arbitrary",),
            vmem_limit_bytes=VMEM_LIMIT_BYTES),
        name="out_ffn_ple",
    )(x, yc, ya, p, wo, gffn, wgu, wd, gple, wpg, bpg, wpp, gfin)


def _layer(h, p_i, mix_norm, w_in, b_f, conv_w, mix_out_norm, w_o, ffn_norm, w_gate_up,
           w_down, ple_norm, w_ple_gate, b_ple_gate, w_ple_proj, out_norm):
    B, S, _ = h.shape
    row = lambda a: a.reshape(1, -1).astype(F32)
    n_main = 3 * CONV_WIDTH + 3 * ATTN_WIDTH
    w = jnp.pad(w_in, ((0, 0), (0, LANES - N_HEADS))).astype(BF16)
    bf = jnp.pad(row(b_f), ((0, 0), (0, LANES - N_HEADS)))
    assert w.shape[1] == n_main + LANES
    gmix = row(mix_out_norm)
    yc, q, k, v, crow = _in_proj(h, row(mix_norm), w, bf, conv_w.astype(F32),
                                 gmix[:, :CONV_WIDTH])
    ya = _attention(q, k, v, crow, gmix[:, CONV_WIDTH:])
    n_tok = B * S
    out = _out_block(
        h.reshape(n_tok, D_MODEL), yc.reshape(n_tok, CONV_WIDTH),
        ya.reshape(n_tok, ATTN_WIDTH), p_i.reshape(n_tok, PLE_DIM),
        w_o.astype(BF16), row(ffn_norm), w_gate_up.astype(BF16), w_down.astype(BF16),
        row(ple_norm), w_ple_gate.astype(BF16), row(b_ple_gate), w_ple_proj.astype(BF16),
        row(out_norm))
    return out.reshape(B, S, D_MODEL)


def kernel(x, p, mix_norm, w_in, b_f, conv_w, mix_out_norm, w_o, ffn_norm, w_gate_up,
           w_down, ple_norm, w_ple_gate, b_ple_gate, w_ple_proj, final_norm):
    depth = p.shape[0]
    assert depth == 1, "the final RMSNorm is fused into the single layer's last kernel"
    return _layer(x, p[0], mix_norm[0], w_in[0], b_f[0], conv_w[0], mix_out_norm[0],
                  w_o[0], ffn_norm[0], w_gate_up[0], w_down[0], ple_norm[0],
                  w_ple_gate[0], b_ple_gate[0], w_ple_proj[0], final_norm)
```

```python
import jax
import jax.numpy as jnp
from jax import lax
from jax.experimental import pallas as pl
from jax.experimental.pallas import tpu as pltpu

D_MODEL = 1024
PLE_DIM = 256
CONV_WIDTH = 512
ATTN_WIDTH = 512
GROUP_DIM = 64
N_HEADS = ATTN_WIDTH // GROUP_DIM
CONV_K = 3
D_FF = 2816
EPS = 1e-6

LANES = 128
SUBLANES = 8
HEADS_PER_BLOCK = LANES // GROUP_DIM
VMEM_LIMIT_BYTES = 56 * 1024 * 1024

TM_IN = 512
TQ = 512
TK = 512
PAIRS_PER_STEP = 2
TILE_UNROLL = 2
DRAIN_UNROLL = 2
TM_OUT = 1024
FF_CHUNK = 256
MASK_VALUE = -1e30

F32 = jnp.float32
BF16 = jnp.bfloat16


def _rms_scale(x):
    return lax.rsqrt(jnp.mean(x * x, axis=-1, keepdims=True) + EPS)


def _dot(a, b):
    return jnp.dot(a, b, preferred_element_type=F32)


def _in_proj_kernel(x_ref, g_ref, w_ref, bf_ref, cw_ref, gmix_ref,
                    yc_ref, q_ref, k_ref, v_ref, crow_ref,
                    conv_carry, c_carry):
    tm = x_ref.shape[0]

    @pl.when(pl.program_id(1) == 0)
    def _():
        conv_carry[...] = jnp.zeros_like(conv_carry)
        c_carry[...] = jnp.zeros_like(c_carry)

    x = x_ref[...]
    xn = (x * _rms_scale(x) * g_ref[...]).astype(BF16)

    cw = CONV_WIDTH
    o_q = 3 * cw
    o_v = o_q + 2 * ATTN_WIDTH
    gate_c = _dot(xn, w_ref[:, cw:2 * cw])
    u = _dot(xn, w_ref[:, 2 * cw:3 * cw])
    zvf = _dot(xn, w_ref[:, o_v:o_v + ATTN_WIDTH + LANES])
    gate_b = _dot(xn, w_ref[:, 0:cw])
    q_ref[...] = (_dot(xn, w_ref[:, o_q:o_q + ATTN_WIDTH]) * (GROUP_DIM ** -0.5)).astype(BF16)
    v_ref[...] = zvf[:, 0:ATTN_WIDTH].astype(BF16)

    gcu = gate_c * u
    carry = conv_carry[...]
    prev1 = pltpu.roll(gcu, 1, axis=0)
    prev2 = pltpu.roll(gcu, 2, axis=0)
    row8 = lax.broadcasted_iota(jnp.int32, (SUBLANES, cw), 0)
    head1 = jnp.where(row8 < 1, pltpu.roll(carry, 1, axis=0), prev1[0:SUBLANES])
    head2 = jnp.where(row8 < 2, pltpu.roll(carry, 2, axis=0), prev2[0:SUBLANES])
    prev1 = jnp.concatenate([head1, prev1[SUBLANES:]], axis=0)
    prev2 = jnp.concatenate([head2, prev2[SUBLANES:]], axis=0)
    conv_carry[...] = gcu[tm - SUBLANES:tm]
    conv = cw_ref[0:1, :] * prev2 + cw_ref[1:2, :] * prev1 + cw_ref[2:3, :] * gcu
    yc = gate_b * conv
    y2 = yc * yc
    lane = lax.broadcasted_iota(jnp.int32, (tm, LANES), 1)
    low = lane < GROUP_DIM
    ms = []
    for c0 in range(0, cw, LANES):
        blk = y2[:, c0:c0 + LANES]
        s_lo = jnp.sum(jnp.where(low, blk, 0.0), axis=-1, keepdims=True)
        s_hi = jnp.sum(jnp.where(low, 0.0, blk), axis=-1, keepdims=True)
        ms.append(jnp.where(low, s_lo, s_hi) * (1.0 / GROUP_DIM))
    ms = jnp.concatenate(ms, axis=1)
    yc_ref[...] = (yc * lax.rsqrt(ms + EPS) * gmix_ref[...]).astype(BF16)
    k_ref[...] = _dot(xn, w_ref[:, o_q + ATTN_WIDTH:o_q + 2 * ATTN_WIDTH]).astype(BF16)

    zf = zvf[:, ATTN_WIDTH:] + bf_ref[...]
    lf = -(jnp.maximum(-zf, 0.0) + jnp.log1p(jnp.exp(-jnp.abs(zf))))
    lft = lf.T[0:N_HEADS, :]
    hi = lft.astype(BF16).astype(F32)
    r1 = lft - hi
    mid = r1.astype(BF16).astype(F32)
    lo = r1 - mid
    parts = jnp.concatenate([hi, mid, lo], axis=0).astype(BF16)
    r_i = lax.broadcasted_iota(jnp.int32, (tm, tm), 0)
    c_i = lax.broadcasted_iota(jnp.int32, (tm, tm), 1)
    triu = jnp.where(r_i <= c_i, 1.0, 0.0).astype(BF16)
    cs = _dot(parts, triu)
    c = c_carry[...] + ((cs[0:N_HEADS] + cs[N_HEADS:2 * N_HEADS]) + cs[2 * N_HEADS:])
    c_carry[...] = jnp.broadcast_to(c[:, tm - 1:tm], c_carry.shape)
    crow_ref[...] = c


def _in_proj(x, g, w, bf, cw, gmix):
    B, S, _ = x.shape
    tm = TM_IN
    n_cols = w.shape[1]
    const = lambda b, i: (0, 0)
    single = dict(pipeline_mode=pl.Buffered(1))
    tok = lambda width: pl.BlockSpec((None, tm, width), lambda b, i: (b, i, 0))
    return pl.pallas_call(
        _in_proj_kernel,
        grid=(B, S // tm),
        in_specs=[
            tok(D_MODEL),
            pl.BlockSpec((1, D_MODEL), const, **single),
            pl.BlockSpec((D_MODEL, n_cols), const, **single),
            pl.BlockSpec((1, LANES), const, **single),
            pl.BlockSpec((CONV_K, CONV_WIDTH), const, **single),
            pl.BlockSpec((1, CONV_WIDTH), const, **single),
        ],
        out_specs=[
            tok(CONV_WIDTH), tok(ATTN_WIDTH), tok(ATTN_WIDTH), tok(ATTN_WIDTH),
            pl.BlockSpec((None, N_HEADS, tm), lambda b, i: (b, 0, i)),
        ],
        out_shape=[
            jax.ShapeDtypeStruct((B, S, CONV_WIDTH), BF16),
            jax.ShapeDtypeStruct((B, S, ATTN_WIDTH), BF16),
            jax.ShapeDtypeStruct((B, S, ATTN_WIDTH), BF16),
            jax.ShapeDtypeStruct((B, S, ATTN_WIDTH), BF16),
            jax.ShapeDtypeStruct((B, N_HEADS, S), F32),
        ],
        scratch_shapes=[
            pltpu.VMEM((SUBLANES, CONV_WIDTH), F32),
            pltpu.VMEM((N_HEADS, TM_IN), F32),
        ],
        compiler_params=pltpu.CompilerParams(
            dimension_semantics=("arbitrary", "arbitrary"),
            vmem_limit_bytes=VMEM_LIMIT_BYTES),
        name="in_proj",
    )(x, g, w, bf, cw, gmix)


def _attn_kernel(q_ref, k_ref, v_ref, crow_ref, gmix_ref, o_ref,
                 s_scr, acc_scr, mrun_scr, m_scr, q2_scr):
    assert TQ == TK, "one key tile per query block sits on the diagonal"
    S = q_ref.shape[0]
    nq = S // TQ
    half = TQ // 2
    pairs = range(PAIRS_PER_STEP)
    heads = range(HEADS_PER_BLOCK)
    head0 = pl.program_id(1) * (PAIRS_PER_STEP * HEADS_PER_BLOCK)
    lane = lax.broadcasted_iota(jnp.int32, (TQ, LANES), 1)
    head_lanes = [(lane >= hh * GROUP_DIM) & (lane < (hh + 1) * GROUP_DIM) for hh in heads]
    ones = jnp.ones((TK, LANES), BF16)
    top_rows = [slice(hh * TQ, hh * TQ + half) for hh in heads]
    bot_rows = [slice(hh * TQ + half, (hh + 1) * TQ) for hh in heads]

    acc_scr[...] = jnp.zeros_like(acc_scr)
    mrun_scr[...] = jnp.full(mrun_scr.shape, MASK_VALUE, F32)

    def crow(pp, hh, start, size):
        return crow_ref[pl.ds(head0 + pp * HEADS_PER_BLOCK + hh, 1), pl.ds(start, size)]

    def lane_block_max(t):
        tmax = t[:, 0:LANES]
        for c0 in range(LANES, t.shape[1], LANES):
            tmax = jnp.maximum(tmax, t[:, c0:c0 + LANES])
        return tmax

    def exp_pv(pp, row_slices, s0, n_keys):
        m = jnp.concatenate([m_scr[pp, r, :] for r in row_slices], axis=0)
        s = jnp.concatenate([s_scr[pp, r, pl.ds(s0, n_keys)] for r in row_slices], axis=0)
        p = jnp.concatenate(
            [jnp.exp(s[:, c0:c0 + LANES] - m) for c0 in range(0, n_keys, LANES)], axis=1)
        v_aug = jnp.concatenate(
            [v_ref[pl.ds(s0, n_keys), pp * LANES:(pp + 1) * LANES], ones[0:n_keys]], axis=1)
        pv = _dot(p.astype(BF16), v_aug)
        r0 = 0
        for r in row_slices:
            n = r.stop - r.start
            acc_scr[pp, r, :] += pv[r0:r0 + n]
            r0 += n

    def pass2_tile(pp, s0):
        exp_pv(pp, [slice(0, HEADS_PER_BLOCK * TQ)], s0, TK)

    def pass2_diag(pp, s0):
        exp_pv(pp, top_rows, s0, half)
        exp_pv(pp, bot_rows, s0, TK)

    def pass1_tile(pp, s0, crefs):
        raw = lax.dot_general(q2_scr[pp], k_ref[pl.ds(s0, TK), pp * LANES:(pp + 1) * LANES],
                              (((1,), (1,)), ((), ())), preferred_element_type=F32)
        parts = [raw[hh * TQ:(hh + 1) * TQ] + (crefs[pp][hh] - crow(pp, hh, s0, TK))
                 for hh in heads]
        t = jnp.concatenate(parts, axis=0)
        s_scr[pp, :, pl.ds(s0, TK)] = t
        mrun_scr[pp] = jnp.maximum(mrun_scr[pp], lane_block_max(t))

    def pass1_diag(pp, t0, crefs):
        nt = (((1,), (1,)), ((), ()))
        t1 = pl.multiple_of(t0 + half, half)
        k_lo = k_ref[pl.ds(t0, half), pp * LANES:(pp + 1) * LANES]
        k_hi = k_ref[pl.ds(t1, half), pp * LANES:(pp + 1) * LANES]
        raw_lo = lax.dot_general(q2_scr[pp], k_lo, nt, preferred_element_type=F32)
        q_bot = jnp.concatenate([q2_scr[pp, r, :] for r in bot_rows], axis=0)
        raw_hi = lax.dot_general(q_bot, k_hi, nt, preferred_element_type=F32)
        causal = (lax.broadcasted_iota(jnp.int32, (half, half), 1)
                  <= lax.broadcasted_iota(jnp.int32, (half, half), 0))
        for hh in heads:
            bias_lo = crefs[pp][hh] - crow(pp, hh, t0, half)
            bias_hi = crefs[pp][hh] - crow(pp, hh, t1, half)
            top = jnp.where(causal, raw_lo[top_rows[hh]] + bias_lo, MASK_VALUE)
            bot_lo = raw_lo[bot_rows[hh]] + bias_lo
            bot_hi = jnp.where(causal, raw_hi[hh * half:(hh + 1) * half] + bias_hi, MASK_VALUE)
            s_scr[pp, top_rows[hh], pl.ds(t0, half)] = top
            s_scr[pp, bot_rows[hh], pl.ds(t0, half)] = bot_lo
            s_scr[pp, bot_rows[hh], pl.ds(t1, half)] = bot_hi
            mrun_scr[pp, top_rows[hh], :] = jnp.maximum(mrun_scr[pp, top_rows[hh], :],
                                                        lane_block_max(top))
            mrun_scr[pp, bot_rows[hh], :] = jnp.maximum(
                mrun_scr[pp, bot_rows[hh], :],
                jnp.maximum(lane_block_max(bot_lo), lane_block_max(bot_hi)))

    def finalize(t0):
        for pp in pairs:
            out = None
            for hh in heads:
                a = acc_scr[pp, hh * TQ:(hh + 1) * TQ, :]
                o = jnp.where(head_lanes[hh], a[:, 0:LANES] / a[:, LANES:], 0.0)
                ms = jnp.sum(o * o, axis=-1, keepdims=True) * (1.0 / GROUP_DIM)
                o = o * lax.rsqrt(ms + EPS)
                out = o if out is None else out + o
            o_ref[pl.ds(t0, TQ), pp * LANES:(pp + 1) * LANES] = (
                out * gmix_ref[:, pp * LANES:(pp + 1) * LANES]).astype(BF16)
        acc_scr[...] = jnp.zeros_like(acc_scr)

    def q_block(qi, first):
        t0 = pl.multiple_of(qi * TQ, TQ)
        crefs = []
        for pp in pairs:
            q = q_ref[pl.ds(t0, TQ), pp * LANES:(pp + 1) * LANES]
            crefs.append([])
            for hh in heads:
                q2_scr[pp, hh * TQ:(hh + 1) * TQ, :] = jnp.where(head_lanes[hh], q,
                                                                 jnp.zeros_like(q))
                crefs[pp].append(jnp.min(crow(pp, hh, t0, TQ), axis=-1, keepdims=True))

        def both(j, n_tiles):
            starts = [pl.multiple_of((j * n_tiles + d) * TK, TK) for d in range(n_tiles)]
            for s0 in starts:
                for pp in pairs:
                    pass2_tile(pp, s0)
            for s0 in starts:
                for pp in pairs:
                    pass1_tile(pp, s0, crefs)
            return 0

        if not first:
            n_full = qi - 1
            n_steps = n_full // TILE_UNROLL
            lax.fori_loop(0, n_steps, lambda j, _: both(j, TILE_UNROLL), 0)
            lax.fori_loop(n_steps * TILE_UNROLL, n_full, lambda j, _: both(j, 1), 0)
            s_prev = pl.multiple_of(t0 - TQ, TQ)
            for pp in pairs:
                pass2_diag(pp, s_prev)
                pass1_tile(pp, s_prev, crefs)
            finalize(s_prev)

        for pp in pairs:
            pass1_diag(pp, t0, crefs)
        for pp in pairs:
            m = jnp.max(mrun_scr[pp], axis=-1, keepdims=True)
            m_scr[pp] = jnp.broadcast_to(m, m_scr.shape[1:])
        mrun_scr[...] = jnp.full(mrun_scr.shape, MASK_VALUE, F32)
        return 0

    q_block(0, True)
    lax.fori_loop(1, nq, lambda qi, _: q_block(qi, False), 0)

    def drain(j, _):
        for d in range(DRAIN_UNROLL):
            for pp in pairs:
                pass2_tile(pp, pl.multiple_of((j * DRAIN_UNROLL + d) * TK, TK))
        return 0

    n_steps = (nq - 1) // DRAIN_UNROLL
    lax.fori_loop(0, n_steps, drain, 0)
    for j in range(n_steps * DRAIN_UNROLL, nq - 1):
        for pp in pairs:
            pass2_tile(pp, j * TK)
    for pp in pairs:
        pass2_diag(pp, S - TQ)
    finalize(S - TQ)


def _attention(q, k, v, crow, gmix_attn):
    B, S, _ = q.shape
    n_pairs = N_HEADS // HEADS_PER_BLOCK
    rows = HEADS_PER_BLOCK * TQ
    width = PAIRS_PER_STEP * LANES
    seq = pl.BlockSpec((None, S, width), lambda b, j: (b, 0, j))
    return pl.pallas_call(
        _attn_kernel,
        grid=(B, n_pairs // PAIRS_PER_STEP),
        in_specs=[
            seq, seq, seq,
            pl.BlockSpec((None, N_HEADS, S), lambda b, j: (b, 0, 0)),
            pl.BlockSpec((1, width), lambda b, j: (0, j)),
        ],
        out_specs=seq,
        out_shape=jax.ShapeDtypeStruct((B, S, ATTN_WIDTH), BF16),
        scratch_shapes=[
            pltpu.VMEM((PAIRS_PER_STEP, rows, S), F32),
            pltpu.VMEM((PAIRS_PER_STEP, rows, 2 * LANES), F32),
            pltpu.VMEM((PAIRS_PER_STEP, rows, LANES), F32),
            pltpu.VMEM((PAIRS_PER_STEP, rows, LANES), F32),
            pltpu.VMEM((PAIRS_PER_STEP, rows, LANES), BF16),
        ],
        compiler_params=pltpu.CompilerParams(
            dimension_semantics=("arbitrary", "arbitrary"),
            vmem_limit_bytes=VMEM_LIMIT_BYTES),
        name="forgetting_attention",
    )(q, k, v, crow, gmix_attn)


def _out_kernel(x_ref, yc_ref, ya_ref, p_ref, wo_ref, gffn_ref, wgu_ref, wd_ref,
                gple_ref, wpg_ref, bpg_ref, wpp_ref, gfin_ref, o_ref):
    y = jnp.concatenate([yc_ref[...], ya_ref[...]], axis=-1)
    h = x_ref[...] + _dot(y, wo_ref[...])
    hn = (h * _rms_scale(h) * gffn_ref[...]).astype(BF16)
    ff = None
    for c0 in range(0, D_FF, FF_CHUNK):
        g = _dot(hn, wgu_ref[:, c0:c0 + FF_CHUNK])
        up = _dot(hn, wgu_ref[:, D_FF + c0:D_FF + c0 + FF_CHUNK])
        a = (g * jax.nn.sigmoid(g) * up).astype(BF16)
        d = _dot(a, wd_ref[c0:c0 + FF_CHUNK, :])
        ff = d if ff is None else ff + d
    h = h + ff
    hn = (h * _rms_scale(h) * gple_ref[...]).astype(BF16)
    gate = jax.nn.sigmoid(_dot(hn, wpg_ref[...]) + bpg_ref[...])
    h = h + gate * _dot(p_ref[...].astype(BF16), wpp_ref[...])
    o_ref[...] = h * _rms_scale(h) * gfin_ref[...]


def _out_block(x, yc, ya, p, wo, gffn, wgu, wd, gple, wpg, bpg, wpp, gfin):
    n_tok = x.shape[0]
    tm = TM_OUT
    const = lambda i: (0, 0)
    single = dict(pipeline_mode=pl.Buffered(1))
    tok = lambda width: pl.BlockSpec((tm, width), lambda i: (i, 0))
    full = lambda a: pl.BlockSpec(a.shape, const, **single)
    return pl.pallas_call(
        _out_kernel,
        grid=(n_tok // tm,),
        in_specs=[tok(D_MODEL), tok(CONV_WIDTH), tok(ATTN_WIDTH), tok(PLE_DIM),
                  full(wo), full(gffn), full(wgu), full(wd), full(gple), full(wpg),
                  full(bpg), full(wpp), full(gfin)],
        out_specs=tok(D_MODEL),
        out_shape=jax.ShapeDtypeStruct((n_tok, D_MODEL), F32),
        compiler_params=pltpu.CompilerParams(
            dimension_semantics=("arbitrary",),
            vmem_limit_bytes=VMEM_LIMIT_BYTES),
        name="out_ffn_ple",
    )(x, yc, ya, p, wo, gffn, wgu, wd, gple, wpg, bpg, wpp, gfin)


def _layer(h, p_i, mix_norm, w_in, b_f, conv_w, mix_out_norm, w_o, ffn_norm, w_gate_up,
           w_down, ple_norm, w_ple_gate, b_ple_gate, w_ple_proj, out_norm):
    B, S, _ = h.shape
    row = lambda a: a.reshape(1, -1).astype(F32)
    n_main = 3 * CONV_WIDTH + 3 * ATTN_WIDTH
    w = jnp.pad(w_in, ((0, 0), (0, LANES - N_HEADS))).astype(BF16)
    bf = jnp.pad(row(b_f), ((0, 0), (0, LANES - N_HEADS)))
    assert w.shape[1] == n_main + LANES
    gmix = row(mix_out_norm)
    yc, q, k, v, crow = _in_proj(h, row(mix_norm), w, bf, conv_w.astype(F32),
                                 gmix[:, :CONV_WIDTH])
    ya = _attention(q, k, v, crow, gmix[:, CONV_WIDTH:])
    n_tok = B * S
    out = _out_block(
        h.reshape(n_tok, D_MODEL), yc.reshape(n_tok, CONV_WIDTH),
        ya.reshape(n_tok, ATTN_WIDTH), p_i.reshape(n_tok, PLE_DIM),
        w_o.astype(BF16), row(ffn_norm), w_gate_up.astype(BF16), w_down.astype(BF16),
        row(ple_norm), w_ple_gate.astype(BF16), row(b_ple_gate), w_ple_proj.astype(BF16),
        row(out_norm))
    return out.reshape(B, S, D_MODEL)


def kernel(x, p, mix_norm, w_in, b_f, conv_w, mix_out_norm, w_o, ffn_norm, w_gate_up,
           w_down, ple_norm, w_ple_gate, b_ple_gate, w_ple_proj, final_norm):
    depth = p.shape[0]
    assert depth == 1, "the final RMSNorm is fused into the single layer's last kernel"
    return _layer(x, p[0], mix_norm[0], w_in[0], b_f[0], conv_w[0], mix_out_norm[0],
                  w_o[0], ffn_norm[0], w_gate_up[0], w_down[0], ple_norm[0],
                  w_ple_gate[0], b_ple_gate[0], w_ple_proj[0], final_norm)
```

```python
import jax
import jax.numpy as jnp
from jax import lax
from jax.experimental import pallas as pl
from jax.experimental.pallas import tpu as pltpu

D_MODEL = 1024
PLE_DIM = 256
CONV_WIDTH = 512
ATTN_WIDTH = 512
GROUP_DIM = 64
N_HEADS = ATTN_WIDTH // GROUP_DIM
CONV_K = 3
D_FF = 2816
EPS = 1e-6

LANES = 128
SUBLANES = 8
HEADS_PER_BLOCK = LANES // GROUP_DIM
VMEM_LIMIT_BYTES = 56 * 1024 * 1024

TM_IN = 1024
TQ = 512
TK = 512
PAIRS_PER_STEP = 2
TILE_UNROLL = 2
DRAIN_UNROLL = 2
TM_OUT = 1024
FF_CHUNK = 256
MASK_VALUE = -1e30

F32 = jnp.float32
BF16 = jnp.bfloat16


def _rms_scale(x):
    return lax.rsqrt(jnp.mean(x * x, axis=-1, keepdims=True) + EPS)


def _dot(a, b):
    return jnp.dot(a, b, preferred_element_type=F32)


def _in_proj_kernel(x_ref, g_ref, w_ref, bf_ref, cw_ref, gmix_ref,
                    yc_ref, q_ref, k_ref, v_ref, crow_ref,
                    conv_carry, c_carry):
    tm = x_ref.shape[0]

    @pl.when(pl.program_id(1) == 0)
    def _():
        conv_carry[...] = jnp.zeros_like(conv_carry)
        c_carry[...] = jnp.zeros_like(c_carry)

    x = x_ref[...]
    xn = (x * _rms_scale(x) * g_ref[...]).astype(BF16)

    cw = CONV_WIDTH
    o_q = 3 * cw
    o_v = o_q + 2 * ATTN_WIDTH
    gate_c = _dot(xn, w_ref[:, cw:2 * cw])
    u = _dot(xn, w_ref[:, 2 * cw:3 * cw])
    zvf = _dot(xn, w_ref[:, o_v:o_v + ATTN_WIDTH + LANES])
    gate_b = _dot(xn, w_ref[:, 0:cw])
    q_ref[...] = (_dot(xn, w_ref[:, o_q:o_q + ATTN_WIDTH]) * (GROUP_DIM ** -0.5)).astype(BF16)
    v_ref[...] = zvf[:, 0:ATTN_WIDTH].astype(BF16)

    gcu = gate_c * u
    carry = conv_carry[...]
    prev1 = pltpu.roll(gcu, 1, axis=0)
    prev2 = pltpu.roll(gcu, 2, axis=0)
    row8 = lax.broadcasted_iota(jnp.int32, (SUBLANES, cw), 0)
    head1 = jnp.where(row8 < 1, pltpu.roll(carry, 1, axis=0), prev1[0:SUBLANES])
    head2 = jnp.where(row8 < 2, pltpu.roll(carry, 2, axis=0), prev2[0:SUBLANES])
    prev1 = jnp.concatenate([head1, prev1[SUBLANES:]], axis=0)
    prev2 = jnp.concatenate([head2, prev2[SUBLANES:]], axis=0)
    conv_carry[...] = gcu[tm - SUBLANES:tm]
    conv = cw_ref[0:1, :] * prev2 + cw_ref[1:2, :] * prev1 + cw_ref[2:3, :] * gcu
    yc = gate_b * conv
    y2 = yc * yc
    lane = lax.broadcasted_iota(jnp.int32, (tm, LANES), 1)
    low = lane < GROUP_DIM
    ms = []
    for c0 in range(0, cw, LANES):
        blk = y2[:, c0:c0 + LANES]
        s_lo = jnp.sum(jnp.where(low, blk, 0.0), axis=-1, keepdims=True)
        s_hi = jnp.sum(jnp.where(low, 0.0, blk), axis=-1, keepdims=True)
        ms.append(jnp.where(low, s_lo, s_hi) * (1.0 / GROUP_DIM))
    ms = jnp.concatenate(ms, axis=1)
    yc_ref[...] = (yc * lax.rsqrt(ms + EPS) * gmix_ref[...]).astype(BF16)
    k_ref[...] = _dot(xn, w_ref[:, o_q + ATTN_WIDTH:o_q + 2 * ATTN_WIDTH]).astype(BF16)

    zf = zvf[:, ATTN_WIDTH:] + bf_ref[...]
    lf = -(jnp.maximum(-zf, 0.0) + jnp.log1p(jnp.exp(-jnp.abs(zf))))
    lft = lf.T[0:N_HEADS, :]
    hi = lft.astype(BF16).astype(F32)
    r1 = lft - hi
    mid = r1.astype(BF16).astype(F32)
    lo = r1 - mid
    parts = jnp.concatenate([hi, mid, lo], axis=0).astype(BF16)
    r_i = lax.broadcasted_iota(jnp.int32, (tm, tm), 0)
    c_i = lax.broadcasted_iota(jnp.int32, (tm, tm), 1)
    triu = jnp.where(r_i <= c_i, 1.0, 0.0).astype(BF16)
    cs = _dot(parts, triu)
    c = c_carry[...] + ((cs[0:N_HEADS] + cs[N_HEADS:2 * N_HEADS]) + cs[2 * N_HEADS:])
    c_carry[...] = jnp.broadcast_to(c[:, tm - 1:tm], c_carry.shape)
    crow_ref[...] = c


def _in_proj(x, g, w, bf, cw, gmix):
    B, S, _ = x.shape
    tm = TM_IN
    n_cols = w.shape[1]
    const = lambda b, i: (0, 0)
    single = dict(pipeline_mode=pl.Buffered(1))
    tok = lambda width: pl.BlockSpec((None, tm, width), lambda b, i: (b, i, 0))
    return pl.pallas_call(
        _in_proj_kernel,
        grid=(B, S // tm),
        in_specs=[
            tok(D_MODEL),
            pl.BlockSpec((1, D_MODEL), const, **single),
            pl.BlockSpec((D_MODEL, n_cols), const, **single),
            pl.BlockSpec((1, LANES), const, **single),
            pl.BlockSpec((CONV_K, CONV_WIDTH), const, **single),
            pl.BlockSpec((1, CONV_WIDTH), const, **single),
        ],
        out_specs=[
            tok(CONV_WIDTH), tok(ATTN_WIDTH), tok(ATTN_WIDTH), tok(ATTN_WIDTH),
            pl.BlockSpec((None, N_HEADS, tm), lambda b, i: (b, 0, i)),
        ],
        out_shape=[
            jax.ShapeDtypeStruct((B, S, CONV_WIDTH), BF16),
            jax.ShapeDtypeStruct((B, S, ATTN_WIDTH), BF16),
            jax.ShapeDtypeStruct((B, S, ATTN_WIDTH), BF16),
            jax.ShapeDtypeStruct((B, S, ATTN_WIDTH), BF16),
            jax.ShapeDtypeStruct((B, N_HEADS, S), F32),
        ],
        scratch_shapes=[
            pltpu.VMEM((SUBLANES, CONV_WIDTH), F32),
            pltpu.VMEM((N_HEADS, TM_IN), F32),
        ],
        compiler_params=pltpu.CompilerParams(
            dimension_semantics=("arbitrary", "arbitrary"),
            vmem_limit_bytes=VMEM_LIMIT_BYTES),
        name="in_proj",
    )(x, g, w, bf, cw, gmix)


def _attn_kernel(q_ref, k_ref, v_ref, crow_ref, gmix_ref, o_ref,
                 s_scr, acc_scr, mrun_scr, m_scr, q2_scr):
    assert TQ == TK, "one key tile per query block sits on the diagonal"
    S = q_ref.shape[0]
    nq = S // TQ
    half = TQ // 2
    pairs = range(PAIRS_PER_STEP)
    heads = range(HEADS_PER_BLOCK)
    head0 = pl.program_id(1) * (PAIRS_PER_STEP * HEADS_PER_BLOCK)
    lane = lax.broadcasted_iota(jnp.int32, (TQ, LANES), 1)
    head_lanes = [(lane >= hh * GROUP_DIM) & (lane < (hh + 1) * GROUP_DIM) for hh in heads]
    ones = jnp.ones((TK, LANES), BF16)
    top_rows = [slice(hh * TQ, hh * TQ + half) for hh in heads]
    bot_rows = [slice(hh * TQ + half, (hh + 1) * TQ) for hh in heads]

    acc_scr[...] = jnp.zeros_like(acc_scr)
    mrun_scr[...] = jnp.full(mrun_scr.shape, MASK_VALUE, F32)

    def crow(pp, hh, start, size):
        return crow_ref[pl.ds(head0 + pp * HEADS_PER_BLOCK + hh, 1), pl.ds(start, size)]

    def lane_block_max(t):
        tmax = t[:, 0:LANES]
        for c0 in range(LANES, t.shape[1], LANES):
            tmax = jnp.maximum(tmax, t[:, c0:c0 + LANES])
        return tmax

    def exp_pv(pp, row_slices, s0, n_keys):
        m = jnp.concatenate([m_scr[pp, r, :] for r in row_slices], axis=0)
        s = jnp.concatenate([s_scr[pp, r, pl.ds(s0, n_keys)] for r in row_slices], axis=0)
        p = jnp.concatenate(
            [jnp.exp(s[:, c0:c0 + LANES] - m) for c0 in range(0, n_keys, LANES)], axis=1)
        v_aug = jnp.concatenate(
            [v_ref[pl.ds(s0, n_keys), pp * LANES:(pp + 1) * LANES], ones[0:n_keys]], axis=1)
        pv = _dot(p.astype(BF16), v_aug)
        r0 = 0
        for r in row_slices:
            n = r.stop - r.start
            acc_scr[pp, r, :] += pv[r0:r0 + n]
            r0 += n

    def pass2_tile(pp, s0):
        exp_pv(pp, [slice(0, HEADS_PER_BLOCK * TQ)], s0, TK)

    def pass2_diag(pp, s0):
        exp_pv(pp, top_rows, s0, half)
        exp_pv(pp, bot_rows, s0, TK)

    def pass1_tile(pp, s0, crefs):
        raw = lax.dot_general(q2_scr[pp], k_ref[pl.ds(s0, TK), pp * LANES:(pp + 1) * LANES],
                              (((1,), (1,)), ((), ())), preferred_element_type=F32)
        parts = [raw[hh * TQ:(hh + 1) * TQ] + (crefs[pp][hh] - crow(pp, hh, s0, TK))
                 for hh in heads]
        t = jnp.concatenate(parts, axis=0)
        s_scr[pp, :, pl.ds(s0, TK)] = t
        mrun_scr[pp] = jnp.maximum(mrun_scr[pp], lane_block_max(t))

    def pass1_diag(pp, t0, crefs):
        nt = (((1,), (1,)), ((), ()))
        t1 = pl.multiple_of(t0 + half, half)
        k_lo = k_ref[pl.ds(t0, half), pp * LANES:(pp + 1) * LANES]
        k_hi = k_ref[pl.ds(t1, half), pp * LANES:(pp + 1) * LANES]
        raw_lo = lax.dot_general(q2_scr[pp], k_lo, nt, preferred_element_type=F32)
        q_bot = jnp.concatenate([q2_scr[pp, r, :] for r in bot_rows], axis=0)
        raw_hi = lax.dot_general(q_bot, k_hi, nt, preferred_element_type=F32)
        causal = (lax.broadcasted_iota(jnp.int32, (half, half), 1)
                  <= lax.broadcasted_iota(jnp.int32, (half, half), 0))
        for hh in heads:
            bias_lo = crefs[pp][hh] - crow(pp, hh, t0, half)
            bias_hi = crefs[pp][hh] - crow(pp, hh, t1, half)
            top = jnp.where(causal, raw_lo[top_rows[hh]] + bias_lo, MASK_VALUE)
            bot_lo = raw_lo[bot_rows[hh]] + bias_lo
            bot_hi = jnp.where(causal, raw_hi[hh * half:(hh + 1) * half] + bias_hi, MASK_VALUE)
            s_scr[pp, top_rows[hh], pl.ds(t0, half)] = top
            s_scr[pp, bot_rows[hh], pl.ds(t0, half)] = bot_lo
            s_scr[pp, bot_rows[hh], pl.ds(t1, half)] = bot_hi
            mrun_scr[pp, top_rows[hh], :] = jnp.maximum(mrun_scr[pp, top_rows[hh], :],
                                                        lane_block_max(top))
            mrun_scr[pp, bot_rows[hh], :] = jnp.maximum(
                mrun_scr[pp, bot_rows[hh], :],
                jnp.maximum(lane_block_max(bot_lo), lane_block_max(bot_hi)))

    def finalize(t0):
        for pp in pairs:
            out = None
            for hh in heads:
                a = acc_scr[pp, hh * TQ:(hh + 1) * TQ, :]
                o = jnp.where(head_lanes[hh], a[:, 0:LANES] / a[:, LANES:], 0.0)
                ms = jnp.sum(o * o, axis=-1, keepdims=True) * (1.0 / GROUP_DIM)
                o = o * lax.rsqrt(ms + EPS)
                out = o if out is None else out + o
            o_ref[pl.ds(t0, TQ), pp * LANES:(pp + 1) * LANES] = (
                out * gmix_ref[:, pp * LANES:(pp + 1) * LANES]).astype(BF16)
        acc_scr[...] = jnp.zeros_like(acc_scr)

    def q_block(qi, first):
        t0 = pl.multiple_of(qi * TQ, TQ)
        crefs = []
        for pp in pairs:
            q = q_ref[pl.ds(t0, TQ), pp * LANES:(pp + 1) * LANES]
            crefs.append([])
            for hh in heads:
                q2_scr[pp, hh * TQ:(hh + 1) * TQ, :] = jnp.where(head_lanes[hh], q,
                                                                 jnp.zeros_like(q))
                crefs[pp].append(jnp.min(crow(pp, hh, t0, TQ), axis=-1, keepdims=True))

        def both(j, n_tiles):
            starts = [pl.multiple_of((j * n_tiles + d) * TK, TK) for d in range(n_tiles)]
            for s0 in starts:
                for pp in pairs:
                    pass2_tile(pp, s0)
            for s0 in starts:
                for pp in pairs:
                    pass1_tile(pp, s0, crefs)
            return 0

        if not first:
            n_full = qi - 1
            n_steps = n_full // TILE_UNROLL
            lax.fori_loop(0, n_steps, lambda j, _: both(j, TILE_UNROLL), 0)
            lax.fori_loop(n_steps * TILE_UNROLL, n_full, lambda j, _: both(j, 1), 0)
            s_prev = pl.multiple_of(t0 - TQ, TQ)
            for pp in pairs:
                pass2_diag(pp, s_prev)
                pass1_tile(pp, s_prev, crefs)
            finalize(s_prev)

        for pp in pairs:
            pass1_diag(pp, t0, crefs)
        for pp in pairs:
            m = jnp.max(mrun_scr[pp], axis=-1, keepdims=True)
            m_scr[pp] = jnp.broadcast_to(m, m_scr.shape[1:])
        mrun_scr[...] = jnp.full(mrun_scr.shape, MASK_VALUE, F32)
        return 0

    q_block(0, True)
    lax.fori_loop(1, nq, lambda qi, _: q_block(qi, False), 0)

    def drain(j, _):
        for d in range(DRAIN_UNROLL):
            for pp in pairs:
                pass2_tile(pp, pl.multiple_of((j * DRAIN_UNROLL + d) * TK, TK))
        return 0

    n_steps = (nq - 1) // DRAIN_UNROLL
    lax.fori_loop(0, n_steps, drain, 0)
    for j in range(n_steps * DRAIN_UNROLL, nq - 1):
        for pp in pairs:
            pass2_tile(pp, j * TK)
    for pp in pairs:
        pass2_diag(pp, S - TQ)
    finalize(S - TQ)


def _attention(q, k, v, crow, gmix_attn):
    B, S, _ = q.shape
    n_pairs = N_HEADS // HEADS_PER_BLOCK
    rows = HEADS_PER_BLOCK * TQ
    width = PAIRS_PER_STEP * LANES
    seq = pl.BlockSpec((None, S, width), lambda b, j: (b, 0, j))
    return pl.pallas_call(
        _attn_kernel,
        grid=(B, n_pairs // PAIRS_PER_STEP),
        in_specs=[
            seq, seq, seq,
            pl.BlockSpec((None, N_HEADS, S), lambda b, j: (b, 0, 0)),
            pl.BlockSpec((1, width), lambda b, j: (0, j)),
        ],
        out_specs=seq,
        out_shape=jax.ShapeDtypeStruct((B, S, ATTN_WIDTH), BF16),
        scratch_shapes=[
            pltpu.VMEM((PAIRS_PER_STEP, rows, S), F32),
            pltpu.VMEM((PAIRS_PER_STEP, rows, 2 * LANES), F32),
            pltpu.VMEM((PAIRS_PER_STEP, rows, LANES), F32),
            pltpu.VMEM((PAIRS_PER_STEP, rows, LANES), F32),
            pltpu.VMEM((PAIRS_PER_STEP, rows, LANES), BF16),
        ],
        compiler_params=pltpu.CompilerParams(
            dimension_semantics=("arbitrary", "arbitrary"),
            vmem_limit_bytes=VMEM_LIMIT_BYTES),
        name="forgetting_attention",
    )(q, k, v, crow, gmix_attn)


def _out_kernel(x_ref, yc_ref, ya_ref, p_ref, wo_ref, gffn_ref, wgu_ref, wd_ref,
                gple_ref, wpg_ref, bpg_ref, wpp_ref, gfin_ref, o_ref):
    y = jnp.concatenate([yc_ref[...], ya_ref[...]], axis=-1)
    h = x_ref[...] + _dot(y, wo_ref[...])
    hn = (h * _rms_scale(h) * gffn_ref[...]).astype(BF16)
    ff = None
    for c0 in range(0, D_FF, FF_CHUNK):
        g = _dot(hn, wgu_ref[:, c0:c0 + FF_CHUNK])
        up = _dot(hn, wgu_ref[:, D_FF + c0:D_FF + c0 + FF_CHUNK])
        a = (g * jax.nn.sigmoid(g) * up).astype(BF16)
        d = _dot(a, wd_ref[c0:c0 + FF_CHUNK, :])
        ff = d if ff is None else ff + d
    h = h + ff
    hn = (h * _rms_scale(h) * gple_ref[...]).astype(BF16)
    gate = jax.nn.sigmoid(_dot(hn, wpg_ref[...]) + bpg_ref[...])
    h = h + gate * _dot(p_ref[...].astype(BF16), wpp_ref[...])
    o_ref[...] = h * _rms_scale(h) * gfin_ref[...]


def _out_block(x, yc, ya, p, wo, gffn, wgu, wd, gple, wpg, bpg, wpp, gfin):
    n_tok = x.shape[0]
    tm = TM_OUT
    const = lambda i: (0, 0)
    single = dict(pipeline_mode=pl.Buffered(1))
    tok = lambda width: pl.BlockSpec((tm, width), lambda i: (i, 0))
    full = lambda a: pl.BlockSpec(a.shape, const, **single)
    return pl.pallas_call(
        _out_kernel,
        grid=(n_tok // tm,),
        in_specs=[tok(D_MODEL), tok(CONV_WIDTH), tok(ATTN_WIDTH), tok(PLE_DIM),
                  full(wo), full(gffn), full(wgu), full(wd), full(gple), full(wpg),
                  full(bpg), full(wpp), full(gfin)],
        out_specs=tok(D_MODEL),
        out_shape=jax.ShapeDtypeStruct((n_tok, D_MODEL), F32),
        compiler_params=pltpu.CompilerParams(
            dimension_semantics=("arbitrary",),
            vmem_limit_bytes=VMEM_LIMIT_BYTES),
        name="out_ffn_ple",
    )(x, yc, ya, p, wo, gffn, wgu, wd, gple, wpg, bpg, wpp, gfin)


def _layer(h, p_i, mix_norm, w_in, b_f, conv_w, mix_out_norm, w_o, ffn_norm, w_gate_up,
           w_down, ple_norm, w_ple_gate, b_ple_gate, w_ple_proj, out_norm):
    B, S, _ = h.shape
    row = lambda a: a.reshape(1, -1).astype(F32)
    n_main = 3 * CONV_WIDTH + 3 * ATTN_WIDTH
    w = jnp.pad(w_in, ((0, 0), (0, LANES - N_HEADS))).astype(BF16)
    bf = jnp.pad(row(b_f), ((0, 0), (0, LANES - N_HEADS)))
    assert w.shape[1] == n_main + LANES
    gmix = row(mix_out_norm)
    yc, q, k, v, crow = _in_proj(h, row(mix_norm), w, bf, conv_w.astype(F32),
                                 gmix[:, :CONV_WIDTH])
    ya = _attention(q, k, v, crow, gmix[:, CONV_WIDTH:])
    n_tok = B * S
    out = _out_block(
        h.reshape(n_tok, D_MODEL), yc.reshape(n_tok, CONV_WIDTH),
        ya.reshape(n_tok, ATTN_WIDTH), p_i.reshape(n_tok, PLE_DIM),
        w_o.astype(BF16), row(ffn_norm), w_gate_up.astype(BF16), w_down.astype(BF16),
        row(ple_norm), w_ple_gate.astype(BF16), row(b_ple_gate), w_ple_proj.astype(BF16),
        row(out_norm))
    return out.reshape(B, S, D_MODEL)


def kernel(x, p, mix_norm, w_in, b_f, conv_w, mix_out_norm, w_o, ffn_norm, w_gate_up,
           w_down, ple_norm, w_ple_gate, b_ple_gate, w_ple_proj, final_norm):
    depth = p.shape[0]
    assert depth == 1, "the final RMSNorm is fused into the single layer's last kernel"
    return _layer(x, p[0], mix_norm[0], w_in[0], b_f[0], conv_w[0], mix_out_norm[0],
                  w_o[0], ffn_norm[0], w_gate_up[0], w_down[0], ple_norm[0],
                  w_ple_gate[0], b_ple_gate[0], w_ple_proj[0], final_norm)
```

```python
import jax
import jax.numpy as jnp
from jax import lax
from jax.experimental import pallas as pl
from jax.experimental.pallas import tpu as pltpu

D_MODEL = 1024
PLE_DIM = 256
CONV_WIDTH = 512
ATTN_WIDTH = 512
GROUP_DIM = 64
N_HEADS = ATTN_WIDTH // GROUP_DIM
CONV_K = 3
D_FF = 2816
EPS = 1e-6

LANES = 128
SUBLANES = 8
HEADS_PER_BLOCK = LANES // GROUP_DIM
VMEM_LIMIT_BYTES = 56 * 1024 * 1024

TM_IN = 1024
TQ = 512
TK = 512
PAIRS_PER_STEP = 2
TILE_UNROLLS = (4, 2, 1)
DRAIN_UNROLL = 2
TM_OUT = 1024
FF_CHUNK = 256
MASK_VALUE = -1e30

F32 = jnp.float32
BF16 = jnp.bfloat16


def _rms_scale(x):
    return lax.rsqrt(jnp.mean(x * x, axis=-1, keepdims=True) + EPS)


def _dot(a, b):
    return jnp.dot(a, b, preferred_element_type=F32)


def _in_proj_kernel(x_ref, g_ref, w_ref, bf_ref, cw_ref, gmix_ref,
                    yc_ref, q_ref, k_ref, v_ref, crow_ref,
                    conv_carry, c_carry):
    tm = x_ref.shape[0]

    @pl.when(pl.program_id(1) == 0)
    def _():
        conv_carry[...] = jnp.zeros_like(conv_carry)
        c_carry[...] = jnp.zeros_like(c_carry)

    x = x_ref[...]
    xn = (x * _rms_scale(x) * g_ref[...]).astype(BF16)

    cw = CONV_WIDTH
    o_q = 3 * cw
    o_v = o_q + 2 * ATTN_WIDTH
    gate_c = _dot(xn, w_ref[:, cw:2 * cw])
    u = _dot(xn, w_ref[:, 2 * cw:3 * cw])
    zvf = _dot(xn, w_ref[:, o_v:o_v + ATTN_WIDTH + LANES])
    gate_b = _dot(xn, w_ref[:, 0:cw])
    q_ref[...] = (_dot(xn, w_ref[:, o_q:o_q + ATTN_WIDTH]) * (GROUP_DIM ** -0.5)).astype(BF16)
    v_ref[...] = zvf[:, 0:ATTN_WIDTH].astype(BF16)

    gcu = gate_c * u
    carry = conv_carry[...]
    prev1 = pltpu.roll(gcu, 1, axis=0)
    prev2 = pltpu.roll(gcu, 2, axis=0)
    row8 = lax.broadcasted_iota(jnp.int32, (SUBLANES, cw), 0)
    head1 = jnp.where(row8 < 1, pltpu.roll(carry, 1, axis=0), prev1[0:SUBLANES])
    head2 = jnp.where(row8 < 2, pltpu.roll(carry, 2, axis=0), prev2[0:SUBLANES])
    prev1 = jnp.concatenate([head1, prev1[SUBLANES:]], axis=0)
    prev2 = jnp.concatenate([head2, prev2[SUBLANES:]], axis=0)
    conv_carry[...] = gcu[tm - SUBLANES:tm]
    conv = cw_ref[0:1, :] * prev2 + cw_ref[1:2, :] * prev1 + cw_ref[2:3, :] * gcu
    yc = gate_b * conv
    y2 = yc * yc
    lane = lax.broadcasted_iota(jnp.int32, (tm, LANES), 1)
    low = lane < GROUP_DIM
    ms = []
    for c0 in range(0, cw, LANES):
        blk = y2[:, c0:c0 + LANES]
        s_lo = jnp.sum(jnp.where(low, blk, 0.0), axis=-1, keepdims=True)
        s_hi = jnp.sum(jnp.where(low, 0.0, blk), axis=-1, keepdims=True)
        ms.append(jnp.where(low, s_lo, s_hi) * (1.0 / GROUP_DIM))
    ms = jnp.concatenate(ms, axis=1)
    yc_ref[...] = (yc * lax.rsqrt(ms + EPS) * gmix_ref[...]).astype(BF16)
    k_ref[...] = _dot(xn, w_ref[:, o_q + ATTN_WIDTH:o_q + 2 * ATTN_WIDTH]).astype(BF16)

    zf = zvf[:, ATTN_WIDTH:] + bf_ref[...]
    lf = -(jnp.maximum(-zf, 0.0) + jnp.log1p(jnp.exp(-jnp.abs(zf))))
    lft = lf.T[0:N_HEADS, :]
    hi = lft.astype(BF16).astype(F32)
    r1 = lft - hi
    mid = r1.astype(BF16).astype(F32)
    lo = r1 - mid
    parts = jnp.concatenate([hi, mid, lo], axis=0).astype(BF16)
    r_i = lax.broadcasted_iota(jnp.int32, (tm, tm), 0)
    c_i = lax.broadcasted_iota(jnp.int32, (tm, tm), 1)
    triu = jnp.where(r_i <= c_i, 1.0, 0.0).astype(BF16)
    cs = _dot(parts, triu)
    c = c_carry[...] + ((cs[0:N_HEADS] + cs[N_HEADS:2 * N_HEADS]) + cs[2 * N_HEADS:])
    c_carry[...] = jnp.broadcast_to(c[:, tm - 1:tm], c_carry.shape)
    crow_ref[...] = c


def _in_proj(x, g, w, bf, cw, gmix):
    B, S, _ = x.shape
    tm = TM_IN
    n_cols = w.shape[1]
    const = lambda b, i: (0, 0)
    single = dict(pipeline_mode=pl.Buffered(1))
    tok = lambda width: pl.BlockSpec((None, tm, width), lambda b, i: (b, i, 0))
    return pl.pallas_call(
        _in_proj_kernel,
        grid=(B, S // tm),
        in_specs=[
            tok(D_MODEL),
            pl.BlockSpec((1, D_MODEL), const, **single),
            pl.BlockSpec((D_MODEL, n_cols), const, **single),
            pl.BlockSpec((1, LANES), const, **single),
            pl.BlockSpec((CONV_K, CONV_WIDTH), const, **single),
            pl.BlockSpec((1, CONV_WIDTH), const, **single),
        ],
        out_specs=[
            tok(CONV_WIDTH), tok(ATTN_WIDTH), tok(ATTN_WIDTH), tok(ATTN_WIDTH),
            pl.BlockSpec((None, N_HEADS, tm), lambda b, i: (b, 0, i)),
        ],
        out_shape=[
            jax.ShapeDtypeStruct((B, S, CONV_WIDTH), BF16),
            jax.ShapeDtypeStruct((B, S, ATTN_WIDTH), BF16),
            jax.ShapeDtypeStruct((B, S, ATTN_WIDTH), BF16),
            jax.ShapeDtypeStruct((B, S, ATTN_WIDTH), BF16),
            jax.ShapeDtypeStruct((B, N_HEADS, S), F32),
        ],
        scratch_shapes=[
            pltpu.VMEM((SUBLANES, CONV_WIDTH), F32),
            pltpu.VMEM((N_HEADS, TM_IN), F32),
        ],
        compiler_params=pltpu.CompilerParams(
            dimension_semantics=("arbitrary", "arbitrary"),
            vmem_limit_bytes=VMEM_LIMIT_BYTES),
        name="in_proj",
    )(x, g, w, bf, cw, gmix)


def _attn_kernel(q_ref, k_ref, v_ref, crow_ref, gmix_ref, o_ref,
                 s_scr, acc_scr, mrun_scr, m_scr, q2_scr):
    assert TQ == TK, "one key tile per query block sits on the diagonal"
    S = q_ref.shape[0]
    nq = S // TQ
    half = TQ // 2
    pairs = range(PAIRS_PER_STEP)
    heads = range(HEADS_PER_BLOCK)
    head0 = pl.program_id(1) * (PAIRS_PER_STEP * HEADS_PER_BLOCK)
    lane = lax.broadcasted_iota(jnp.int32, (TQ, LANES), 1)
    head_lanes = [(lane >= hh * GROUP_DIM) & (lane < (hh + 1) * GROUP_DIM) for hh in heads]
    ones = jnp.ones((TK, LANES), BF16)
    top_rows = [slice(hh * TQ, hh * TQ + half) for hh in heads]
    bot_rows = [slice(hh * TQ + half, (hh + 1) * TQ) for hh in heads]

    acc_scr[...] = jnp.zeros_like(acc_scr)
    mrun_scr[...] = jnp.full(mrun_scr.shape, MASK_VALUE, F32)

    def crow(pp, hh, start, size):
        return crow_ref[pl.ds(head0 + pp * HEADS_PER_BLOCK + hh, 1), pl.ds(start, size)]

    def lane_block_max(t):
        tmax = t[:, 0:LANES]
        for c0 in range(LANES, t.shape[1], LANES):
            tmax = jnp.maximum(tmax, t[:, c0:c0 + LANES])
        return tmax

    def exp_pv(pp, row_slices, s0, n_keys):
        m = jnp.concatenate([m_scr[pp, r, :] for r in row_slices], axis=0)
        s = jnp.concatenate([s_scr[pp, r, pl.ds(s0, n_keys)] for r in row_slices], axis=0)
        p = jnp.concatenate(
            [jnp.exp(s[:, c0:c0 + LANES] - m) for c0 in range(0, n_keys, LANES)], axis=1)
        v_aug = jnp.concatenate(
            [v_ref[pl.ds(s0, n_keys), pp * LANES:(pp + 1) * LANES], ones[0:n_keys]], axis=1)
        pv = _dot(p.astype(BF16), v_aug)
        r0 = 0
        for r in row_slices:
            n = r.stop - r.start
            acc_scr[pp, r, :] += pv[r0:r0 + n]
            r0 += n

    def pass2_tile(pp, s0):
        exp_pv(pp, [slice(0, HEADS_PER_BLOCK * TQ)], s0, TK)

    def pass2_diag(pp, s0):
        exp_pv(pp, top_rows, s0, half)
        exp_pv(pp, bot_rows, s0, TK)

    def pass1_tile(pp, s0, crefs):
        raw = lax.dot_general(q2_scr[pp], k_ref[pl.ds(s0, TK), pp * LANES:(pp + 1) * LANES],
                              (((1,), (1,)), ((), ())), preferred_element_type=F32)
        parts = [raw[hh * TQ:(hh + 1) * TQ] + (crefs[pp][hh] - crow(pp, hh, s0, TK))
                 for hh in heads]
        t = jnp.concatenate(parts, axis=0)
        s_scr[pp, :, pl.ds(s0, TK)] = t
        mrun_scr[pp] = jnp.maximum(mrun_scr[pp], lane_block_max(t))

    def pass1_diag(pp, t0, crefs):
        nt = (((1,), (1,)), ((), ()))
        t1 = pl.multiple_of(t0 + half, half)
        k_lo = k_ref[pl.ds(t0, half), pp * LANES:(pp + 1) * LANES]
        k_hi = k_ref[pl.ds(t1, half), pp * LANES:(pp + 1) * LANES]
        raw_lo = lax.dot_general(q2_scr[pp], k_lo, nt, preferred_element_type=F32)
        q_bot = jnp.concatenate([q2_scr[pp, r, :] for r in bot_rows], axis=0)
        raw_hi = lax.dot_general(q_bot, k_hi, nt, preferred_element_type=F32)
        causal = (lax.broadcasted_iota(jnp.int32, (half, half), 1)
                  <= lax.broadcasted_iota(jnp.int32, (half, half), 0))
        for hh in heads:
            bias_lo = crefs[pp][hh] - crow(pp, hh, t0, half)
            bias_hi = crefs[pp][hh] - crow(pp, hh, t1, half)
            top = jnp.where(causal, raw_lo[top_rows[hh]] + bias_lo, MASK_VALUE)
            bot_lo = raw_lo[bot_rows[hh]] + bias_lo
            bot_hi = jnp.where(causal, raw_hi[hh * half:(hh + 1) * half] + bias_hi, MASK_VALUE)
            s_scr[pp, top_rows[hh], pl.ds(t0, half)] = top
            s_scr[pp, bot_rows[hh], pl.ds(t0, half)] = bot_lo
            s_scr[pp, bot_rows[hh], pl.ds(t1, half)] = bot_hi
            mrun_scr[pp, top_rows[hh], :] = jnp.maximum(mrun_scr[pp, top_rows[hh], :],
                                                        lane_block_max(top))
            mrun_scr[pp, bot_rows[hh], :] = jnp.maximum(
                mrun_scr[pp, bot_rows[hh], :],
                jnp.maximum(lane_block_max(bot_lo), lane_block_max(bot_hi)))

    def finalize(t0):
        for pp in pairs:
            out = None
            for hh in heads:
                a = acc_scr[pp, hh * TQ:(hh + 1) * TQ, :]
                o = jnp.where(head_lanes[hh], a[:, 0:LANES] / a[:, LANES:], 0.0)
                ms = jnp.sum(o * o, axis=-1, keepdims=True) * (1.0 / GROUP_DIM)
                o = o * lax.rsqrt(ms + EPS)
                out = o if out is None else out + o
            o_ref[pl.ds(t0, TQ), pp * LANES:(pp + 1) * LANES] = (
                out * gmix_ref[:, pp * LANES:(pp + 1) * LANES]).astype(BF16)
        acc_scr[...] = jnp.zeros_like(acc_scr)

    def q_block(qi, first):
        t0 = pl.multiple_of(qi * TQ, TQ)
        crefs = []
        for pp in pairs:
            q = q_ref[pl.ds(t0, TQ), pp * LANES:(pp + 1) * LANES]
            crefs.append([])
            for hh in heads:
                q2_scr[pp, hh * TQ:(hh + 1) * TQ, :] = jnp.where(head_lanes[hh], q,
                                                                 jnp.zeros_like(q))
                crefs[pp].append(jnp.min(crow(pp, hh, t0, TQ), axis=-1, keepdims=True))

        def both(j, n_tiles, first_tile):
            starts = [pl.multiple_of((first_tile + j * n_tiles + d) * TK, TK)
                      for d in range(n_tiles)]
            for s0 in starts:
                for pp in pairs:
                    pass2_tile(pp, s0)
            for s0 in starts:
                for pp in pairs:
                    pass1_tile(pp, s0, crefs)
            return 0

        if not first:
            n_full = qi - 1
            done = 0
            for unroll in TILE_UNROLLS:
                n_steps = (n_full - done) // unroll
                lax.fori_loop(0, n_steps, lambda j, _, u=unroll, d=done: both(j, u, d), 0)
                done = done + n_steps * unroll
            s_prev = pl.multiple_of(t0 - TQ, TQ)
            for pp in pairs:
                pass2_diag(pp, s_prev)
                pass1_tile(pp, s_prev, crefs)
            finalize(s_prev)

        for pp in pairs:
            pass1_diag(pp, t0, crefs)
        for pp in pairs:
            m = jnp.max(mrun_scr[pp], axis=-1, keepdims=True)
            m_scr[pp] = jnp.broadcast_to(m, m_scr.shape[1:])
        mrun_scr[...] = jnp.full(mrun_scr.shape, MASK_VALUE, F32)
        return 0

    q_block(0, True)
    lax.fori_loop(1, nq, lambda qi, _: q_block(qi, False), 0)

    def drain(j, _):
        for d in range(DRAIN_UNROLL):
            for pp in pairs:
                pass2_tile(pp, pl.multiple_of((j * DRAIN_UNROLL + d) * TK, TK))
        return 0

    n_steps = (nq - 1) // DRAIN_UNROLL
    lax.fori_loop(0, n_steps, drain, 0)
    for j in range(n_steps * DRAIN_UNROLL, nq - 1):
        for pp in pairs:
            pass2_tile(pp, j * TK)
    for pp in pairs:
        pass2_diag(pp, S - TQ)
    finalize(S - TQ)


def _attention(q, k, v, crow, gmix_attn):
    B, S, _ = q.shape
    n_pairs = N_HEADS // HEADS_PER_BLOCK
    rows = HEADS_PER_BLOCK * TQ
    width = PAIRS_PER_STEP * LANES
    seq = pl.BlockSpec((None, S, width), lambda b, j: (b, 0, j))
    return pl.pallas_call(
        _attn_kernel,
        grid=(B, n_pairs // PAIRS_PER_STEP),
        in_specs=[
            seq, seq, seq,
            pl.BlockSpec((None, N_HEADS, S), lambda b, j: (b, 0, 0)),
            pl.BlockSpec((1, width), lambda b, j: (0, j)),
        ],
        out_specs=seq,
        out_shape=jax.ShapeDtypeStruct((B, S, ATTN_WIDTH), BF16),
        scratch_shapes=[
            pltpu.VMEM((PAIRS_PER_STEP, rows, S), F32),
            pltpu.VMEM((PAIRS_PER_STEP, rows, 2 * LANES), F32),
            pltpu.VMEM((PAIRS_PER_STEP, rows, LANES), F32),
            pltpu.VMEM((PAIRS_PER_STEP, rows, LANES), F32),
            pltpu.VMEM((PAIRS_PER_STEP, rows, LANES), BF16),
        ],
        compiler_params=pltpu.CompilerParams(
            dimension_semantics=("arbitrary", "arbitrary"),
            vmem_limit_bytes=VMEM_LIMIT_BYTES),
        name="forgetting_attention",
    )(q, k, v, crow, gmix_attn)


def _out_kernel(x_ref, yc_ref, ya_ref, p_ref, wo_ref, gffn_ref, wgu_ref, wd_ref,
                gple_ref, wpg_ref, bpg_ref, wpp_ref, gfin_ref, o_ref):
    y = jnp.concatenate([yc_ref[...], ya_ref[...]], axis=-1)
    h = x_ref[...] + _dot(y, wo_ref[...])
    hn = (h * _rms_scale(h) * gffn_ref[...]).astype(BF16)
    ff = None
    for c0 in range(0, D_FF, FF_CHUNK):
        g = _dot(hn, wgu_ref[:, c0:c0 + FF_CHUNK])
        up = _dot(hn, wgu_ref[:, D_FF + c0:D_FF + c0 + FF_CHUNK])
        a = (g * jax.nn.sigmoid(g) * up).astype(BF16)
        d = _dot(a, wd_ref[c0:c0 + FF_CHUNK, :])
        ff = d if ff is None else ff + d
    h = h + ff
    hn = (h * _rms_scale(h) * gple_ref[...]).astype(BF16)
    gate = jax.nn.sigmoid(_dot(hn, wpg_ref[...]) + bpg_ref[...])
    h = h + gate * _dot(p_ref[...].astype(BF16), wpp_ref[...])
    o_ref[...] = h * _rms_scale(h) * gfin_ref[...]


def _out_block(x, yc, ya, p, wo, gffn, wgu, wd, gple, wpg, bpg, wpp, gfin):
    n_tok = x.shape[0]
    tm = TM_OUT
    const = lambda i: (0, 0)
    single = dict(pipeline_mode=pl.Buffered(1))
    tok = lambda width: pl.BlockSpec((tm, width), lambda i: (i, 0))
    full = lambda a: pl.BlockSpec(a.shape, const, **single)
    return pl.pallas_call(
        _out_kernel,
        grid=(n_tok // tm,),
        in_specs=[tok(D_MODEL), tok(CONV_WIDTH), tok(ATTN_WIDTH), tok(PLE_DIM),
                  full(wo), full(gffn), full(wgu), full(wd), full(gple), full(wpg),
                  full(bpg), full(wpp), full(gfin)],
        out_specs=tok(D_MODEL),
        out_shape=jax.ShapeDtypeStruct((n_tok, D_MODEL), F32),
        compiler_params=pltpu.CompilerParams(
            dimension_semantics=("arbitrary",),
            vmem_limit_bytes=VMEM_LIMIT_BYTES),
        name="out_ffn_ple",
    )(x, yc, ya, p, wo, gffn, wgu, wd, gple, wpg, bpg, wpp, gfin)


def _layer(h, p_i, mix_norm, w_in, b_f, conv_w, mix_out_norm, w_o, ffn_norm, w_gate_up,
           w_down, ple_norm, w_ple_gate, b_ple_gate, w_ple_proj, out_norm):
    B, S, _ = h.shape
    row = lambda a: a.reshape(1, -1).astype(F32)
    n_main = 3 * CONV_WIDTH + 3 * ATTN_WIDTH
    w = jnp.pad(w_in, ((0, 0), (0, LANES - N_HEADS))).astype(BF16)
    bf = jnp.pad(row(b_f), ((0, 0), (0, LANES - N_HEADS)))
    assert w.shape[1] == n_main + LANES
    gmix = row(mix_out_norm)
    yc, q, k, v, crow = _in_proj(h, row(mix_norm), w, bf, conv_w.astype(F32),
                                 gmix[:, :CONV_WIDTH])
    ya = _attention(q, k, v, crow, gmix[:, CONV_WIDTH:])
    n_tok = B * S
    out = _out_block(
        h.reshape(n_tok, D_MODEL), yc.reshape(n_tok, CONV_WIDTH),
        ya.reshape(n_tok, ATTN_WIDTH), p_i.reshape(n_tok, PLE_DIM),
        w_o.astype(BF16), row(ffn_norm), w_gate_up.astype(BF16), w_down.astype(BF16),
        row(ple_norm), w_ple_gate.astype(BF16), row(b_ple_gate), w_ple_proj.astype(BF16),
        row(out_norm))
    return out.reshape(B, S, D_MODEL)


def kernel(x, p, mix_norm, w_in, b_f, conv_w, mix_out_norm, w_o, ffn_norm, w_gate_up,
           w_down, ple_norm, w_ple_gate, b_ple_gate, w_ple_proj, final_norm):
    depth = p.shape[0]
    assert depth == 1, "the final RMSNorm is fused into the single layer's last kernel"
    return _layer(x, p[0], mix_norm[0], w_in[0], b_f[0], conv_w[0], mix_out_norm[0],
                  w_o[0], ffn_norm[0], w_gate_up[0], w_down[0], ple_norm[0],
                  w_ple_gate[0], b_ple_gate[0], w_ple_proj[0], final_norm)
```

```python
import jax
import jax.numpy as jnp
from jax import lax
from jax.experimental import pallas as pl
from jax.experimental.pallas import tpu as pltpu

D_MODEL = 1024
PLE_DIM = 256
CONV_WIDTH = 512
ATTN_WIDTH = 512
GROUP_DIM = 64
N_HEADS = ATTN_WIDTH // GROUP_DIM
CONV_K = 3
D_FF = 2816
EPS = 1e-6

LANES = 128
SUBLANES = 8
HEADS_PER_BLOCK = LANES // GROUP_DIM
VMEM_LIMIT_BYTES = 56 * 1024 * 1024

TM_IN = 1024
TQ = 512
TK = 512
PAIRS_PER_STEP = 2
TILE_UNROLLS = (4, 2, 1)
DRAIN_UNROLL = 2
TM_OUT = 512
FF_CHUNK = 256
MASK_VALUE = -1e30

F32 = jnp.float32
BF16 = jnp.bfloat16


def _rms_scale(x):
    return lax.rsqrt(jnp.mean(x * x, axis=-1, keepdims=True) + EPS)


def _dot(a, b):
    return jnp.dot(a, b, preferred_element_type=F32)


def _in_proj_kernel(x_ref, g_ref, w_ref, bf_ref, cw_ref, gmix_ref,
                    yc_ref, q_ref, k_ref, v_ref, crow_ref,
                    conv_carry, c_carry):
    tm = x_ref.shape[0]

    @pl.when(pl.program_id(1) == 0)
    def _():
        conv_carry[...] = jnp.zeros_like(conv_carry)
        c_carry[...] = jnp.zeros_like(c_carry)

    x = x_ref[...]
    xn = (x * _rms_scale(x) * g_ref[...]).astype(BF16)

    cw = CONV_WIDTH
    o_q = 3 * cw
    o_v = o_q + 2 * ATTN_WIDTH
    gate_c = _dot(xn, w_ref[:, cw:2 * cw])
    u = _dot(xn, w_ref[:, 2 * cw:3 * cw])
    zvf = _dot(xn, w_ref[:, o_v:o_v + ATTN_WIDTH + LANES])
    gate_b = _dot(xn, w_ref[:, 0:cw])
    q_ref[...] = (_dot(xn, w_ref[:, o_q:o_q + ATTN_WIDTH]) * (GROUP_DIM ** -0.5)).astype(BF16)
    v_ref[...] = zvf[:, 0:ATTN_WIDTH].astype(BF16)

    gcu = gate_c * u
    carry = conv_carry[...]
    prev1 = pltpu.roll(gcu, 1, axis=0)
    prev2 = pltpu.roll(gcu, 2, axis=0)
    row8 = lax.broadcasted_iota(jnp.int32, (SUBLANES, cw), 0)
    head1 = jnp.where(row8 < 1, pltpu.roll(carry, 1, axis=0), prev1[0:SUBLANES])
    head2 = jnp.where(row8 < 2, pltpu.roll(carry, 2, axis=0), prev2[0:SUBLANES])
    prev1 = jnp.concatenate([head1, prev1[SUBLANES:]], axis=0)
    prev2 = jnp.concatenate([head2, prev2[SUBLANES:]], axis=0)
    conv_carry[...] = gcu[tm - SUBLANES:tm]
    conv = cw_ref[0:1, :] * prev2 + cw_ref[1:2, :] * prev1 + cw_ref[2:3, :] * gcu
    yc = gate_b * conv
    y2 = yc * yc
    lane = lax.broadcasted_iota(jnp.int32, (tm, LANES), 1)
    low = lane < GROUP_DIM
    ms = []
    for c0 in range(0, cw, LANES):
        blk = y2[:, c0:c0 + LANES]
        s_lo = jnp.sum(jnp.where(low, blk, 0.0), axis=-1, keepdims=True)
        s_hi = jnp.sum(jnp.where(low, 0.0, blk), axis=-1, keepdims=True)
        ms.append(jnp.where(low, s_lo, s_hi) * (1.0 / GROUP_DIM))
    ms = jnp.concatenate(ms, axis=1)
    yc_ref[...] = (yc * lax.rsqrt(ms + EPS) * gmix_ref[...]).astype(BF16)
    k_ref[...] = _dot(xn, w_ref[:, o_q + ATTN_WIDTH:o_q + 2 * ATTN_WIDTH]).astype(BF16)

    zf = zvf[:, ATTN_WIDTH:] + bf_ref[...]
    lf = -(jnp.maximum(-zf, 0.0) + jnp.log1p(jnp.exp(-jnp.abs(zf))))
    lft = lf.T[0:N_HEADS, :]
    hi = lft.astype(BF16).astype(F32)
    r1 = lft - hi
    mid = r1.astype(BF16).astype(F32)
    lo = r1 - mid
    parts = jnp.concatenate([hi, mid, lo], axis=0).astype(BF16)
    r_i = lax.broadcasted_iota(jnp.int32, (tm, tm), 0)
    c_i = lax.broadcasted_iota(jnp.int32, (tm, tm), 1)
    triu = jnp.where(r_i <= c_i, 1.0, 0.0).astype(BF16)
    cs = _dot(parts, triu)
    c = c_carry[...] + ((cs[0:N_HEADS] + cs[N_HEADS:2 * N_HEADS]) + cs[2 * N_HEADS:])
    c_carry[...] = jnp.broadcast_to(c[:, tm - 1:tm], c_carry.shape)
    crow_ref[...] = c


def _in_proj(x, g, w, bf, cw, gmix):
    B, S, _ = x.shape
    tm = TM_IN
    n_cols = w.shape[1]
    const = lambda b, i: (0, 0)
    single = dict(pipeline_mode=pl.Buffered(1))
    tok = lambda width: pl.BlockSpec((None, tm, width), lambda b, i: (b, i, 0))
    return pl.pallas_call(
        _in_proj_kernel,
        grid=(B, S // tm),
        in_specs=[
            tok(D_MODEL),
            pl.BlockSpec((1, D_MODEL), const, **single),
            pl.BlockSpec((D_MODEL, n_cols), const, **single),
            pl.BlockSpec((1, LANES), const, **single),
            pl.BlockSpec((CONV_K, CONV_WIDTH), const, **single),
            pl.BlockSpec((1, CONV_WIDTH), const, **single),
        ],
        out_specs=[
            tok(CONV_WIDTH), tok(ATTN_WIDTH), tok(ATTN_WIDTH), tok(ATTN_WIDTH),
            pl.BlockSpec((None, N_HEADS, tm), lambda b, i: (b, 0, i)),
        ],
        out_shape=[
            jax.ShapeDtypeStruct((B, S, CONV_WIDTH), BF16),
            jax.ShapeDtypeStruct((B, S, ATTN_WIDTH), BF16),
            jax.ShapeDtypeStruct((B, S, ATTN_WIDTH), BF16),
            jax.ShapeDtypeStruct((B, S, ATTN_WIDTH), BF16),
            jax.ShapeDtypeStruct((B, N_HEADS, S), F32),
        ],
        scratch_shapes=[
            pltpu.VMEM((SUBLANES, CONV_WIDTH), F32),
            pltpu.VMEM((N_HEADS, TM_IN), F32),
        ],
        compiler_params=pltpu.CompilerParams(
            dimension_semantics=("arbitrary", "arbitrary"),
            vmem_limit_bytes=VMEM_LIMIT_BYTES),
        name="in_proj",
    )(x, g, w, bf, cw, gmix)


def _attn_kernel(q_ref, k_ref, v_ref, crow_ref, gmix_ref, o_ref,
                 s_scr, acc_scr, mrun_scr, m_scr, q2_scr):
    assert TQ == TK, "one key tile per query block sits on the diagonal"
    S = q_ref.shape[0]
    nq = S // TQ
    half = TQ // 2
    pairs = range(PAIRS_PER_STEP)
    heads = range(HEADS_PER_BLOCK)
    head0 = pl.program_id(1) * (PAIRS_PER_STEP * HEADS_PER_BLOCK)
    lane = lax.broadcasted_iota(jnp.int32, (TQ, LANES), 1)
    head_lanes = [(lane >= hh * GROUP_DIM) & (lane < (hh + 1) * GROUP_DIM) for hh in heads]
    ones = jnp.ones((TK, LANES), BF16)
    top_rows = [slice(hh * TQ, hh * TQ + half) for hh in heads]
    bot_rows = [slice(hh * TQ + half, (hh + 1) * TQ) for hh in heads]

    acc_scr[...] = jnp.zeros_like(acc_scr)
    mrun_scr[...] = jnp.full(mrun_scr.shape, MASK_VALUE, F32)

    def crow(pp, hh, start, size):
        return crow_ref[pl.ds(head0 + pp * HEADS_PER_BLOCK + hh, 1), pl.ds(start, size)]

    def lane_block_max(t):
        tmax = t[:, 0:LANES]
        for c0 in range(LANES, t.shape[1], LANES):
            tmax = jnp.maximum(tmax, t[:, c0:c0 + LANES])
        return tmax

    def exp_pv(pp, row_slices, s0, n_keys):
        m = jnp.concatenate([m_scr[pp, r, :] for r in row_slices], axis=0)
        s = jnp.concatenate([s_scr[pp, r, pl.ds(s0, n_keys)] for r in row_slices], axis=0)
        p = jnp.concatenate(
            [jnp.exp(s[:, c0:c0 + LANES] - m) for c0 in range(0, n_keys, LANES)], axis=1)
        v_aug = jnp.concatenate(
            [v_ref[pl.ds(s0, n_keys), pp * LANES:(pp + 1) * LANES], ones[0:n_keys]], axis=1)
        pv = _dot(p.astype(BF16), v_aug)
        r0 = 0
        for r in row_slices:
            n = r.stop - r.start
            acc_scr[pp, r, :] += pv[r0:r0 + n]
            r0 += n

    def pass2_tile(pp, s0):
        exp_pv(pp, [slice(0, HEADS_PER_BLOCK * TQ)], s0, TK)

    def pass2_diag(pp, s0):
        exp_pv(pp, top_rows, s0, half)
        exp_pv(pp, bot_rows, s0, TK)

    def pass1_tile(pp, s0, crefs):
        raw = lax.dot_general(q2_scr[pp], k_ref[pl.ds(s0, TK), pp * LANES:(pp + 1) * LANES],
                              (((1,), (1,)), ((), ())), preferred_element_type=F32)
        parts = [raw[hh * TQ:(hh + 1) * TQ] + (crefs[pp][hh] - crow(pp, hh, s0, TK))
                 for hh in heads]
        t = jnp.concatenate(parts, axis=0)
        s_scr[pp, :, pl.ds(s0, TK)] = t
        mrun_scr[pp] = jnp.maximum(mrun_scr[pp], lane_block_max(t))

    def pass1_diag(pp, t0, crefs):
        nt = (((1,), (1,)), ((), ()))
        t1 = pl.multiple_of(t0 + half, half)
        k_lo = k_ref[pl.ds(t0, half), pp * LANES:(pp + 1) * LANES]
        k_hi = k_ref[pl.ds(t1, half), pp * LANES:(pp + 1) * LANES]
        raw_lo = lax.dot_general(q2_scr[pp], k_lo, nt, preferred_element_type=F32)
        q_bot = jnp.concatenate([q2_scr[pp, r, :] for r in bot_rows], axis=0)
        raw_hi = lax.dot_general(q_bot, k_hi, nt, preferred_element_type=F32)
        causal = (lax.broadcasted_iota(jnp.int32, (half, half), 1)
                  <= lax.broadcasted_iota(jnp.int32, (half, half), 0))
        for hh in heads:
            bias_lo = crefs[pp][hh] - crow(pp, hh, t0, half)
            bias_hi = crefs[pp][hh] - crow(pp, hh, t1, half)
            top = jnp.where(causal, raw_lo[top_rows[hh]] + bias_lo, MASK_VALUE)
            bot_lo = raw_lo[bot_rows[hh]] + bias_lo
            bot_hi = jnp.where(causal, raw_hi[hh * half:(hh + 1) * half] + bias_hi, MASK_VALUE)
            s_scr[pp, top_rows[hh], pl.ds(t0, half)] = top
            s_scr[pp, bot_rows[hh], pl.ds(t0, half)] = bot_lo
            s_scr[pp, bot_rows[hh], pl.ds(t1, half)] = bot_hi
            mrun_scr[pp, top_rows[hh], :] = jnp.maximum(mrun_scr[pp, top_rows[hh], :],
                                                        lane_block_max(top))
            mrun_scr[pp, bot_rows[hh], :] = jnp.maximum(
                mrun_scr[pp, bot_rows[hh], :],
                jnp.maximum(lane_block_max(bot_lo), lane_block_max(bot_hi)))

    def finalize(t0):
        for pp in pairs:
            out = None
            for hh in heads:
                a = acc_scr[pp, hh * TQ:(hh + 1) * TQ, :]
                o = jnp.where(head_lanes[hh], a[:, 0:LANES] / a[:, LANES:], 0.0)
                ms = jnp.sum(o * o, axis=-1, keepdims=True) * (1.0 / GROUP_DIM)
                o = o * lax.rsqrt(ms + EPS)
                out = o if out is None else out + o
            o_ref[pl.ds(t0, TQ), pp * LANES:(pp + 1) * LANES] = (
                out * gmix_ref[:, pp * LANES:(pp + 1) * LANES]).astype(BF16)
        acc_scr[...] = jnp.zeros_like(acc_scr)

    def q_block(qi, first):
        t0 = pl.multiple_of(qi * TQ, TQ)
        crefs = []
        for pp in pairs:
            q = q_ref[pl.ds(t0, TQ), pp * LANES:(pp + 1) * LANES]
            crefs.append([])
            for hh in heads:
                q2_scr[pp, hh * TQ:(hh + 1) * TQ, :] = jnp.where(head_lanes[hh], q,
                                                                 jnp.zeros_like(q))
                crefs[pp].append(jnp.min(crow(pp, hh, t0, TQ), axis=-1, keepdims=True))

        def both(j, n_tiles, first_tile):
            starts = [pl.multiple_of((first_tile + j * n_tiles + d) * TK, TK)
                      for d in range(n_tiles)]
            for s0 in starts:
                for pp in pairs:
                    pass2_tile(pp, s0)
            for s0 in starts:
                for pp in pairs:
                    pass1_tile(pp, s0, crefs)
            return 0

        if not first:
            n_full = qi - 1
            done = 0
            for unroll in TILE_UNROLLS:
                n_steps = (n_full - done) // unroll
                lax.fori_loop(0, n_steps, lambda j, _, u=unroll, d=done: both(j, u, d), 0)
                done = done + n_steps * unroll
            s_prev = pl.multiple_of(t0 - TQ, TQ)
            for pp in pairs:
                pass2_diag(pp, s_prev)
                pass1_tile(pp, s_prev, crefs)
            finalize(s_prev)

        for pp in pairs:
            pass1_diag(pp, t0, crefs)
        for pp in pairs:
            m = jnp.max(mrun_scr[pp], axis=-1, keepdims=True)
            m_scr[pp] = jnp.broadcast_to(m, m_scr.shape[1:])
        mrun_scr[...] = jnp.full(mrun_scr.shape, MASK_VALUE, F32)
        return 0

    q_block(0, True)
    lax.fori_loop(1, nq, lambda qi, _: q_block(qi, False), 0)

    def drain(j, _):
        for d in range(DRAIN_UNROLL):
            for pp in pairs:
                pass2_tile(pp, pl.multiple_of((j * DRAIN_UNROLL + d) * TK, TK))
        return 0

    n_steps = (nq - 1) // DRAIN_UNROLL
    lax.fori_loop(0, n_steps, drain, 0)
    for j in range(n_steps * DRAIN_UNROLL, nq - 1):
        for pp in pairs:
            pass2_tile(pp, j * TK)
    for pp in pairs:
        pass2_diag(pp, S - TQ)
    finalize(S - TQ)


def _attention(q, k, v, crow, gmix_attn):
    B, S, _ = q.shape
    n_pairs = N_HEADS // HEADS_PER_BLOCK
    rows = HEADS_PER_BLOCK * TQ
    width = PAIRS_PER_STEP * LANES
    seq = pl.BlockSpec((None, S, width), lambda b, j: (b, 0, j))
    return pl.pallas_call(
        _attn_kernel,
        grid=(B, n_pairs // PAIRS_PER_STEP),
        in_specs=[
            seq, seq, seq,
            pl.BlockSpec((None, N_HEADS, S), lambda b, j: (b, 0, 0)),
            pl.BlockSpec((1, width), lambda b, j: (0, j)),
        ],
        out_specs=seq,
        out_shape=jax.ShapeDtypeStruct((B, S, ATTN_WIDTH), BF16),
        scratch_shapes=[
            pltpu.VMEM((PAIRS_PER_STEP, rows, S), F32),
            pltpu.VMEM((PAIRS_PER_STEP, rows, 2 * LANES), F32),
            pltpu.VMEM((PAIRS_PER_STEP, rows, LANES), F32),
            pltpu.VMEM((PAIRS_PER_STEP, rows, LANES), F32),
            pltpu.VMEM((PAIRS_PER_STEP, rows, LANES), BF16),
        ],
        compiler_params=pltpu.CompilerParams(
            dimension_semantics=("arbitrary", "arbitrary"),
            vmem_limit_bytes=VMEM_LIMIT_BYTES),
        name="forgetting_attention",
    )(q, k, v, crow, gmix_attn)


def _out_kernel(x_ref, yc_ref, ya_ref, p_ref, wo_ref, gffn_ref, wgu_ref, wd_ref,
                gple_ref, wpg_ref, bpg_ref, wpp_ref, gfin_ref, o_ref, h_scr, hn_scr):
    i = pl.program_id(0)
    cur = i % 2
    prev = 1 - cur

    @pl.when(i == 0)
    def _():
        h_scr[1] = jnp.zeros(h_scr.shape[1:], F32)
        hn_scr[1] = jnp.zeros(hn_scr.shape[1:], BF16)

    hn = hn_scr[prev]
    ff = None
    for c0 in range(0, D_FF, FF_CHUNK):
        g = _dot(hn, wgu_ref[:, c0:c0 + FF_CHUNK])
        up = _dot(hn, wgu_ref[:, D_FF + c0:D_FF + c0 + FF_CHUNK])
        a = (g * jax.nn.sigmoid(g) * up).astype(BF16)
        d = _dot(a, wd_ref[c0:c0 + FF_CHUNK, :])
        ff = d if ff is None else ff + d
    h = h_scr[prev] + ff

    y = jnp.concatenate([yc_ref[...], ya_ref[...]], axis=-1)
    h_new = x_ref[...] + _dot(y, wo_ref[...])
    h_scr[cur] = h_new
    hn_scr[cur] = (h_new * _rms_scale(h_new) * gffn_ref[...]).astype(BF16)

    hn = (h * _rms_scale(h) * gple_ref[...]).astype(BF16)
    gate = jax.nn.sigmoid(_dot(hn, wpg_ref[...]) + bpg_ref[...])
    h = h + gate * _dot(p_ref[...].astype(BF16), wpp_ref[...])
    o_ref[...] = h * _rms_scale(h) * gfin_ref[...]


def _out_block(x, yc, ya, p, wo, gffn, wgu, wd, gple, wpg, bpg, wpp, gfin):
    n_tok = x.shape[0]
    tm = TM_OUT
    n_tiles = n_tok // tm
    const = lambda i: (0, 0)
    single = dict(pipeline_mode=pl.Buffered(1))
    started = lambda width: pl.BlockSpec((tm, width), lambda i: (jnp.minimum(i, n_tiles - 1), 0))
    finished = lambda width: pl.BlockSpec((tm, width), lambda i: (jnp.maximum(i - 1, 0), 0))
    full = lambda a: pl.BlockSpec(a.shape, const, **single)
    return pl.pallas_call(
        _out_kernel,
        grid=(n_tiles + 1,),
        in_specs=[started(D_MODEL), started(CONV_WIDTH), started(ATTN_WIDTH), finished(PLE_DIM),
                  full(wo), full(gffn), full(wgu), full(wd), full(gple), full(wpg),
                  full(bpg), full(wpp), full(gfin)],
        out_specs=finished(D_MODEL),
        out_shape=jax.ShapeDtypeStruct((n_tok, D_MODEL), F32),
        scratch_shapes=[
            pltpu.VMEM((2, tm, D_MODEL), F32),
            pltpu.VMEM((2, tm, D_MODEL), BF16),
        ],
        compiler_params=pltpu.CompilerParams(
            dimension_semantics=("arbitrary",),
            vmem_limit_bytes=VMEM_LIMIT_BYTES),
        name="out_ffn_ple",
    )(x, yc, ya, p, wo, gffn, wgu, wd, gple, wpg, bpg, wpp, gfin)


def _layer(h, p_i, mix_norm, w_in, b_f, conv_w, mix_out_norm, w_o, ffn_norm, w_gate_up,
           w_down, ple_norm, w_ple_gate, b_ple_gate, w_ple_proj, out_norm):
    B, S, _ = h.shape
    row = lambda a: a.reshape(1, -1).astype(F32)
    n_main = 3 * CONV_WIDTH + 3 * ATTN_WIDTH
    w = jnp.pad(w_in, ((0, 0), (0, LANES - N_HEADS))).astype(BF16)
    bf = jnp.pad(row(b_f), ((0, 0), (0, LANES - N_HEADS)))
    assert w.shape[1] == n_main + LANES
    gmix = row(mix_out_norm)
    yc, q, k, v, crow = _in_proj(h, row(mix_norm), w, bf, conv_w.astype(F32),
                                 gmix[:, :CONV_WIDTH])
    ya = _attention(q, k, v, crow, gmix[:, CONV_WIDTH:])
    n_tok = B * S
    out = _out_block(
        h.reshape(n_tok, D_MODEL), yc.reshape(n_tok, CONV_WIDTH),
        ya.reshape(n_tok, ATTN_WIDTH), p_i.reshape(n_tok, PLE_DIM),
        w_o.astype(BF16), row(ffn_norm), w_gate_up.astype(BF16), w_down.astype(BF16),
        row(ple_norm), w_ple_gate.astype(BF16), row(b_ple_gate), w_ple_proj.astype(BF16),
        row(out_norm))
    return out.reshape(B, S, D_MODEL)


def kernel(x, p, mix_norm, w_in, b_f, conv_w, mix_out_norm, w_o, ffn_norm, w_gate_up,
           w_down, ple_norm, w_ple_gate, b_ple_gate, w_ple_proj, final_norm):
    depth = p.shape[0]
    assert depth == 1, "the final RMSNorm is fused into the single layer's last kernel"
    return _layer(x, p[0], mix_norm[0], w_in[0], b_f[0], conv_w[0], mix_out_norm[0],
                  w_o[0], ffn_norm[0], w_gate_up[0], w_down[0], ple_norm[0],
                  w_ple_gate[0], b_ple_gate[0], w_ple_proj[0], final_norm)
```
